```python
import math
import jax, jax.numpy as jnp
from jax import lax
import numpy as np

D_MODEL = 1024
BATCH = 8
SEQ = 8192
DEPTH = 2

GRID_W = 64
N_MEM = 256
N_BRANCH = 4
BRANCH_W = 512
NA_HEADS = 8
NA_HEAD_DIM = 64
NA_WIN_R = 8
NA_WIN_C = 16
HY_W = 512
HY_CONV = 3
HY_EMB_BANDS = 16
HY_EMB = 1 + 2 * HY_EMB_BANDS
HY_FILTER_HID = 64
HY_INNER_MLPS = 2
HY_SIN_FREQ = 1.0
HY_DECAY_TARGET = 1e-2
HY_FAST_DECAY_PCT = 0.3
HY_SLOW_DECAY_PCT = 1.5
HY_MAX_DECAY = math.log(HY_DECAY_TARGET) / HY_FAST_DECAY_PCT
HY_MIN_DECAY = math.log(HY_DECAY_TARGET) / HY_SLOW_DECAY_PCT
GLA_HEADS = 4
GLA_DK = 64
GLA_DV = 128
GLA_RANK = 16
GLA_TAU = 16.0
GLA_CHUNK = 64
MEM_HEADS = 4
MEM_HEAD_DIM = 128
LN_EPS = 1e-5
RMS_EPS = 1e-6
DN_ALPHA = (2 * DEPTH) ** 0.25
DN_BETA = (8 * DEPTH) ** -0.25

IN_SIZES = (
    NA_HEADS * NA_HEAD_DIM, NA_HEADS * NA_HEAD_DIM, NA_HEADS * NA_HEAD_DIM,
    3 * HY_W,
    GLA_HEADS * GLA_DK, GLA_HEADS * GLA_DK, GLA_HEADS * GLA_DV,
    GLA_RANK, GLA_RANK,
    MEM_HEADS * MEM_HEAD_DIM,
    N_BRANCH * BRANCH_W,
    N_BRANCH * D_MODEL,
)
N_IN = sum(IN_SIZES)

kernel_name = "hybrid_na_hyena_gla_mem_encoder"


def layer_norm(x, g, b):
    xf = x.astype(jnp.float32)
    mu = jnp.mean(xf, -1, keepdims=True)
    var = jnp.mean(jnp.square(xf - mu), -1, keepdims=True)
    return ((xf - mu) * lax.rsqrt(var + LN_EPS) * g.astype(jnp.float32) + b.astype(jnp.float32)).astype(x.dtype)


def neighbourhood_attention(q, k, v, rpb):
    B, L, H, dh = q.shape
    rows = L // GRID_W
    wr = min(NA_WIN_R, rows)
    wc = NA_WIN_C
    grid = lambda t: t.reshape(B, rows, GRID_W, H, dh).transpose(1, 0, 3, 2, 4)
    qg = grid(q) * (dh ** -0.5)
    kg = grid(k)
    vg = grid(v)
    cols = jnp.arange(GRID_W)
    col_start = jnp.clip(cols - wc // 2, 0, GRID_W - wc)
    col_idx = col_start[:, None] + jnp.arange(wc)[None, :]
    dcol = col_idx - cols[:, None] + NA_WIN_C - 1
    rpb32 = rpb.astype(jnp.float32)

    def row_block(args):
        r, q_r = args
        rs = jnp.clip(r - wr // 2, 0, rows - wr)
        k_sel = lax.dynamic_slice_in_dim(kg, rs, wr, axis=0)[:, :, :, col_idx, :]
        v_sel = lax.dynamic_slice_in_dim(vg, rs, wr, axis=0)[:, :, :, col_idx, :]
        drow = rs + jnp.arange(wr) - r + NA_WIN_R - 1
        bias = rpb32[:, drow[:, None, None], dcol[None, :, :]]
        s = jnp.einsum('bhqd,ibhqjd->bhqij', q_r, k_sel).astype(jnp.float32)
        s = s + bias.transpose(0, 2, 1, 3)[None]
        p = jax.nn.softmax(s.reshape(B, H, GRID_W, wr * wc), axis=-1)
        p = p.reshape(B, H, GRID_W, wr, wc).astype(v.dtype)
        return jnp.einsum('bhqij,ibhqjd->bhqd', p, v_sel)

    out = lax.map(row_block, (jnp.arange(rows), qg))
    return out.transpose(1, 0, 3, 2, 4).reshape(B, L, H * dh)


def short_conv(u, w, b):
    L = u.shape[1]
    pad = HY_CONV // 2
    up = jnp.pad(u, ((0, 0), (pad, pad), (0, 0)))
    out = b
    for j in range(HY_CONV):
        out = out + up[:, j:j + L] * w[j]
    return out


def hyena_implicit_filter(L, w1, b1, wm, bm, w3):
    f32 = jnp.float32
    n = jnp.arange(L, dtype=f32)
    t01 = n / (L - 1)
    freqs = jnp.linspace(1e-4, HY_EMB_BANDS - 1, HY_EMB_BANDS, dtype=f32)
    ang = (2.0 * math.pi / L) * n[:, None] * freqs[None, :]
    z = jnp.concatenate([t01[:, None], jnp.cos(ang), -jnp.sin(ang)], axis=-1)
    h = jnp.sin(HY_SIN_FREQ * (z @ w1.astype(f32) + b1.astype(f32)))
    for j in range(HY_INNER_MLPS):
        h = jnp.sin(HY_SIN_FREQ * (h @ wm[j].astype(f32) + bm[j].astype(f32)))
    h = h @ w3.astype(f32)
    deltas = jnp.abs(jnp.linspace(HY_MIN_DECAY, HY_MAX_DECAY, HY_W, dtype=f32))
    window = jnp.exp(-t01[:, None] * deltas[None, :])
    h_f = h[:, :HY_W] * window
    h_b = h[:, HY_W:] * window
    energy = jnp.sum(jnp.square(h_f), 0) + jnp.sum(jnp.square(h_b[1:]), 0)
    scale = lax.rsqrt(energy + 1e-6)
    return h_f * scale, h_b * scale


def bidirectional_long_conv(u, h_f, h_b, skip):
    B, L, C = u.shape
    kern = jnp.concatenate([h_f, jnp.zeros((1, C), jnp.float32), h_b[1:][::-1]], axis=0)
    uf = jnp.fft.rfft(u.astype(jnp.float32), n=2 * L, axis=1)
    kf = jnp.fft.rfft(kern, n=2 * L, axis=0)
    y = jnp.fft.irfft(uf * kf[None], n=2 * L, axis=1)[:, :L]
    return (y + skip.astype(jnp.float32) * u.astype(jnp.float32)).astype(u.dtype)


def gla_chunked(q, k, v, log_a, strict):
    B, L, H, dk = q.shape
    dv = v.shape[-1]
    C = GLA_CHUNK
    n = L // C
    f32 = jnp.float32
    qc = q.astype(f32).reshape(B, n, C, H, dk)
    kc = k.astype(f32).reshape(B, n, C, H, dk)
    vc = v.astype(f32).reshape(B, n, C, H, dv)
    b = jnp.cumsum(log_a.astype(f32).reshape(B, n, C, H, dk), axis=2)
    b_last = b[:, :, -1:]
    m = b[:, :, C // 2 - 1:C // 2]
    att = jnp.einsum('bnchd,bnshd->bnhcs', qc * jnp.exp(b - m), kc * jnp.exp(m - b))
    mask = jnp.tril(jnp.ones((C, C), bool), -1 if strict else 0)
    att = jnp.where(mask, att, 0.0)
    o_intra = jnp.einsum('bnhcs,bnshe->bnche', att, vc)
    U = jnp.einsum('bnshd,bnshe->bnhde', kc * jnp.exp(b_last - b), vc)
    decay = jnp.exp(b_last[:, :, 0])

    def step(S, inp):
        dec, u = inp
        return dec[..., None] * S + u, S

    S0 = jnp.zeros((B, H, dk, dv), f32)
    _, S_prev = lax.scan(step, S0, (jnp.moveaxis(decay, 1, 0), jnp.moveaxis(U, 1, 0)))
    S_prev = jnp.moveaxis(S_prev, 0, 1)
    o_inter = jnp.einsum('bnchd,bnhde->bnche', qc * jnp.exp(b), S_prev)
    return (o_intra + o_inter).reshape(B, L, H, dv)


def gla_bidirectional(q, k, v, lr_f, lr_b, up_f, bias_f, up_b, bias_b, norm_g):
    B, L, _ = q.shape
    qh = q.reshape(B, L, GLA_HEADS, GLA_DK) * (GLA_DK ** -0.5)
    kh = k.reshape(B, L, GLA_HEADS, GLA_DK)
    vh = v.reshape(B, L, GLA_HEADS, GLA_DV)
    gate = lambda lr, up, bias: (jax.nn.log_sigmoid((lr @ up + bias).astype(jnp.float32)) / GLA_TAU
                                 ).reshape(B, L, GLA_HEADS, GLA_DK)
    la_f = gate(lr_f, up_f, bias_f)
    la_b = gate(lr_b, up_b, bias_b)
    o_f = gla_chunked(qh, kh, vh, la_f, strict=False)
    fl = lambda t: jnp.flip(t, axis=1)
    o_b = fl(gla_chunked(fl(qh), fl(kh), fl(vh), fl(la_b), strict=True))
    o = o_f + o_b
    o = o * lax.rsqrt(jnp.mean(jnp.square(o), -1, keepdims=True) + RMS_EPS)
    return (o.reshape(B, L, GLA_HEADS * GLA_DV) * norm_g.astype(jnp.float32)).astype(q.dtype)


def memory_cross_attention(q, mem, w_kv):
    B, L, _ = q.shape
    M = mem.shape[1]
    k, v = jnp.split(mem @ w_kv, 2, axis=-1)
    qh = q.reshape(B, L, MEM_HEADS, MEM_HEAD_DIM) * (MEM_HEAD_DIM ** -0.5)
    kh = k.reshape(B, M, MEM_HEADS, MEM_HEAD_DIM)
    vh = v.reshape(B, M, MEM_HEADS, MEM_HEAD_DIM)
    s = jnp.einsum('blhd,bmhd->bhlm', qh, kh).astype(jnp.float32)
    p = jax.nn.softmax(s, axis=-1).astype(v.dtype)
    return jnp.einsum('bhlm,bmhd->blhd', p, vh).reshape(B, L, MEM_HEADS * MEM_HEAD_DIM)


def hybrid_layer(x, mem, w_in, b_in, na_rpb, hy_conv_w, hy_conv_b, hy_w1, hy_b1, hy_wm, hy_bm,
                 hy_w3, hy_skip, gla_up_f, gla_bias_f, gla_up_b, gla_bias_b, gla_norm_g,
                 mem_w_kv, w_br, w_out, b_out, ln_g, ln_b):
    B, L, _ = x.shape
    proj = x @ w_in + b_in
    points = np.cumsum(IN_SIZES)[:-1].tolist()
    (na_q, na_k, na_v, hy_in, gla_q, gla_k, gla_v, gla_lr_f, gla_lr_b,
     mem_q, z_all, g_all) = jnp.split(proj, points, axis=-1)
    heads = lambda t: t.reshape(B, L, NA_HEADS, NA_HEAD_DIM)
    o_na = neighbourhood_attention(heads(na_q), heads(na_k), heads(na_v), na_rpb)
    hy_v, hy_x0, hy_x1 = jnp.split(short_conv(hy_in, hy_conv_w, hy_conv_b), 3, axis=-1)
    h_f, h_b = hyena_implicit_filter(L, hy_w1, hy_b1, hy_wm, hy_bm, hy_w3)
    o_hy = hy_x0 * bidirectional_long_conv(hy_x1 * hy_v, h_f, h_b, hy_skip)
    o_gla = gla_bidirectional(gla_q, gla_k, gla_v, gla_lr_f, gla_lr_b,
                              gla_up_f, gla_bias_f, gla_up_b, gla_bias_b, gla_norm_g)
    o_mem = memory_cross_attention(mem_q, mem, mem_w_kv)
    branches = (o_na, o_hy, o_gla, o_mem)
    z = jnp.split(z_all, N_BRANCH, axis=-1)
    g = jnp.split(g_all, N_BRANCH, axis=-1)
    merged = jax.nn.sigmoid(g[0]) * ((branches[0] * jax.nn.silu(z[0])) @ w_br[0])
    for j in range(1, N_BRANCH):
        merged = merged + jax.nn.sigmoid(g[j]) * ((branches[j] * jax.nn.silu(z[j])) @ w_br[j])
    out = merged @ w_out + b_out
    return layer_norm(DN_ALPHA * x + out, ln_g, ln_b)


def setup_inputs(seed: int = 0) -> dict:
    key = jax.random.key(seed)
    ks = jax.random.split(key, 32)
    f32 = jnp.float32
    nrm = lambda k, shape, s: jax.random.normal(k, shape, f32) * s
    return {
        "x": nrm(ks[0], (BATCH, SEQ, D_MODEL), 1.0),
        "mem": nrm(ks[1], (BATCH, N_MEM, D_MODEL), 1.0),
        "ln_in_g": 1.0 + nrm(ks[2], (D_MODEL,), 0.02),
        "ln_in_b": nrm(ks[3], (D_MODEL,), 0.02),
        "w_in": nrm(ks[4], (DEPTH, D_MODEL, N_IN), D_MODEL ** -0.5),
        "b_in": nrm(ks[5], (DEPTH, N_IN), 0.02),
        "na_rpb": nrm(ks[6], (DEPTH, NA_HEADS, 2 * NA_WIN_R - 1, 2 * NA_WIN_C - 1), 0.02),
        "hy_conv_w": nrm(ks[7], (DEPTH, HY_CONV, 3 * HY_W), HY_CONV ** -0.5),
        "hy_conv_b": nrm(ks[8], (DEPTH, 3 * HY_W), 0.02),
        "hy_w1": nrm(ks[9], (DEPTH, HY_EMB, HY_FILTER_HID), HY_EMB ** -0.5),
        "hy_b1": nrm(ks[10], (DEPTH, HY_FILTER_HID), 0.02),
        "hy_wm": nrm(ks[11], (DEPTH, HY_INNER_MLPS, HY_FILTER_HID, HY_FILTER_HID), HY_FILTER_HID ** -0.5),
        "hy_bm": nrm(ks[12], (DEPTH, HY_INNER_MLPS, HY_FILTER_HID), 0.02),
        "hy_w3": nrm(ks[13], (DEPTH, HY_FILTER_HID, 2 * HY_W), HY_FILTER_HID ** -0.5),
        "hy_skip": nrm(ks[14], (DEPTH, HY_W), 0.5),
        "gla_up_f": nrm(ks[15], (DEPTH, GLA_RANK, GLA_HEADS * GLA_DK), GLA_RANK ** -0.5),
        "gla_bias_f": nrm(ks[16], (DEPTH, GLA_HEADS * GLA_DK), 0.02),
        "gla_up_b": nrm(ks[17], (DEPTH, GLA_RANK, GLA_HEADS * GLA_DK), GLA_RANK ** -0.5),
        "gla_bias_b": nrm(ks[18], (DEPTH, GLA_HEADS * GLA_DK), 0.02),
        "gla_norm_g": 1.0 + nrm(ks[19], (DEPTH, GLA_HEADS * GLA_DV), 0.02),
        "mem_w_kv": nrm(ks[20], (DEPTH, D_MODEL, 2 * MEM_HEADS * MEM_HEAD_DIM), D_MODEL ** -0.5),
        "w_br": nrm(ks[21], (DEPTH, N_BRANCH, BRANCH_W, D_MODEL), DN_BETA * BRANCH_W ** -0.5),
        "w_out": nrm(ks[22], (DEPTH, D_MODEL, D_MODEL), DN_BETA * D_MODEL ** -0.5),
        "b_out": nrm(ks[23], (DEPTH, D_MODEL), 0.02),
        "ln_g": 1.0 + nrm(ks[24], (DEPTH, D_MODEL), 0.02),
        "ln_b": nrm(ks[25], (DEPTH, D_MODEL), 0.02),
    }


def reference(x, mem, ln_in_g, ln_in_b, w_in, b_in, na_rpb, hy_conv_w, hy_conv_b, hy_w1, hy_b1,
              hy_wm, hy_bm, hy_w3, hy_skip, gla_up_f, gla_bias_f, gla_up_b, gla_bias_b, gla_norm_g,
              mem_w_kv, w_br, w_out, b_out, ln_g, ln_b):
    h = layer_norm(x, ln_in_g, ln_in_b)
    for i in range(DEPTH):
        h = hybrid_layer(h, mem, w_in[i], b_in[i], na_rpb[i], hy_conv_w[i], hy_conv_b[i],
                         hy_w1[i], hy_b1[i], hy_wm[i], hy_bm[i], hy_w3[i], hy_skip[i],
                         gla_up_f[i], gla_bias_f[i], gla_up_b[i], gla_bias_b[i], gla_norm_g[i],
                         mem_w_kv[i], w_br[i], w_out[i], b_out[i], ln_g[i], ln_b[i])
    return h
```

```python
import functools
import math

import numpy as np
import jax
import jax.numpy as jnp
from jax import lax
from jax.experimental import pallas as pl
from jax.experimental.pallas import tpu as pltpu

F32 = jnp.float32
BF16 = jnp.bfloat16

GRID_W = 64
NA_HEADS, NA_HEAD_DIM, NA_WIN_R, NA_WIN_C = 8, 64, 8, 16
HY_W, HY_CONV, HY_EMB_BANDS, HY_FILTER_HID, HY_INNER_MLPS = 512, 3, 16, 64, 2
HY_EMB = 1 + 2 * HY_EMB_BANDS
HY_SIN_FREQ = 1.0
HY_DECAY_TARGET, HY_FAST_DECAY_PCT, HY_SLOW_DECAY_PCT = 1e-2, 0.3, 1.5
HY_MAX_DECAY = math.log(HY_DECAY_TARGET) / HY_FAST_DECAY_PCT
HY_MIN_DECAY = math.log(HY_DECAY_TARGET) / HY_SLOW_DECAY_PCT
GLA_HEADS, GLA_DK, GLA_DV, GLA_RANK, GLA_TAU, GLA_CHUNK = 4, 64, 128, 16, 16.0, 64
MEM_HEADS, MEM_HEAD_DIM = 4, 128
N_BRANCH, BRANCH_W = 4, 512
LN_EPS, RMS_EPS = 1e-5, 1e-6

LANES = 128
VMEM_LIMIT = 52 * 1024 * 1024
NEG_BIG = -1e30

COL_G = 0
COL_Z = 4096
COL_NAQ, COL_NAK, COL_NAV = 6144, 6656, 7168
COL_HYV, COL_HYX0, COL_HYX1 = 7680, 8192, 8704
COL_GQ, COL_GK, COL_GV = 9216, 9472, 9728
COL_MQ = 10240
N_MAIN = 10752
LR_PAD = 128

NA_G = 4
NA_KR = NA_G + NA_WIN_R - 1
GLA_T = 256
DFT_N2 = 128


def _cp(sem):
    return pltpu.CompilerParams(dimension_semantics=sem, vmem_limit_bytes=VMEM_LIMIT)


def _split_bf16(a):
    hi = a.astype(BF16)
    lo = (a - hi.astype(F32)).astype(BF16)
    return hi, lo


def _dot(a, b):
    return jnp.dot(a, b, preferred_element_type=F32)


def _dot3(a, b):
    ah, al = _split_bf16(a)
    bh, bl = _split_bf16(b)
    return _dot(ah, bh) + _dot(al, bh) + _dot(ah, bl)


def _dot_nt(a, b):
    return lax.dot_general(a, b, (((1,), (1,)), ((), ())), preferred_element_type=F32)


def _dot_tn(a, b):
    return lax.dot_general(a, b, (((0,), (0,)), ((), ())), preferred_element_type=F32)


def _layer_norm_rows(y, g, b):
    mu = jnp.mean(y, axis=-1, keepdims=True)
    d = y - mu
    var = jnp.mean(d * d, axis=-1, keepdims=True)
    return d * lax.rsqrt(var + LN_EPS) * g + b


def _ln_in_body(x_ref, g_ref, b_ref, h32_ref, h16_ref):
    h = _layer_norm_rows(x_ref[...], g_ref[...], b_ref[...])
    h32_ref[...] = h
    h16_ref[...] = h.astype(BF16)


def ln_in(x2, g, b, tm=512):
    M, D = x2.shape
    row = pl.BlockSpec((tm, D), lambda i: (i, 0))
    vec = pl.BlockSpec((1, D), lambda i: (0, 0))
    return pl.pallas_call(
        _ln_in_body,
        grid=(M // tm,),
        in_specs=[row, vec, vec],
        out_specs=[row, row],
        out_shape=[jax.ShapeDtypeStruct((M, D), F32), jax.ShapeDtypeStruct((M, D), BF16)],
        compiler_params=_cp(("parallel",)),
        name="ln_in",
    )(x2, g.reshape(1, D), b.reshape(1, D))


def _in_proj_body(x_ref, w_ref, b_ref, wlr_ref, blr_ref, p_ref, lr_ref):
    x = x_ref[...]
    p_ref[...] = (_dot(x, w_ref[...]) + b_ref[...]).astype(BF16)

    @pl.when(pl.program_id(1) == 0)
    def _():
        lr_ref[...] = _dot(x, wlr_ref[...]) + blr_ref[...]


def in_proj(h16, wm, bm, wlr, blr, tm=2048, tn=1536):
    M, D = h16.shape
    NM = wm.shape[1]
    return pl.pallas_call(
        _in_proj_body,
        grid=(M // tm, NM // tn),
        in_specs=[
            pl.BlockSpec((tm, D), lambda i, j: (i, 0)),
            pl.BlockSpec((D, tn), lambda i, j: (0, j)),
            pl.BlockSpec((1, tn), lambda i, j: (0, j)),
            pl.BlockSpec((D, LR_PAD), lambda i, j: (0, 0)),
            pl.BlockSpec((1, LR_PAD), lambda i, j: (0, 0)),
        ],
        out_specs=[
            pl.BlockSpec((tm, tn), lambda i, j: (i, j)),
            pl.BlockSpec((tm, LR_PAD), lambda i, j: (i, 0)),
        ],
        out_shape=[jax.ShapeDtypeStruct((M, NM), BF16), jax.ShapeDtypeStruct((M, LR_PAD), F32)],
        compiler_params=_cp(("parallel", "arbitrary")),
        name="in_proj",
    )(h16, wm, bm, wlr, blr)


def _matmul_body(x_ref, w_ref, o_ref):
    o_ref[...] = _dot(x_ref[...], w_ref[...]).astype(o_ref.dtype)


def matmul_bf16(x, w, tm=512):
    M, K = x.shape
    N = w.shape[1]
    return pl.pallas_call(
        _matmul_body,
        grid=(M // tm,),
        in_specs=[pl.BlockSpec((tm, K), lambda i: (i, 0)), pl.BlockSpec((K, N), lambda i: (0, 0))],
        out_specs=pl.BlockSpec((tm, N), lambda i: (i, 0)),
        out_shape=jax.ShapeDtypeStruct((M, N), BF16),
        compiler_params=_cp(("parallel",)),
        name="mem_kv",
    )(x, w)


def _na_row_start(r, rows):
    return min(max(r - NA_WIN_R // 2, 0), rows - NA_WIN_R)


def _na_group_geometry(g, rows):
    r0 = g * NA_G
    kb = min(max(r0 - NA_WIN_R // 2, 0), rows - NA_KR)
    pat = []
    for i in range(NA_G):
        rs = _na_row_start(r0 + i, rows)
        pat.append(tuple((kb + j - (r0 + i) + NA_WIN_R - 1) if rs <= kb + j < rs + NA_WIN_R else None
                         for j in range(NA_KR)))
    return r0, kb, tuple(pat)


def _na_bias_body(rpb_ref, o_ref, *, pattern):
    h = pl.program_id(0)
    n_dc = 2 * NA_WIN_C - 1
    base = h * ((2 * NA_WIN_R - 1) * n_dc)
    c = lax.broadcasted_iota(jnp.int32, (GRID_W, LANES), 0)
    lane = lax.broadcasted_iota(jnp.int32, (GRID_W, LANES), 1)
    cc = lane % GRID_W
    left = lane < GRID_W
    cs = jnp.clip(c - NA_WIN_C // 2, 0, GRID_W - NA_WIN_C)
    col_ok = (cc >= cs) & (cc < cs + NA_WIN_C)
    dcol = cc - c + NA_WIN_C - 1
    neg = jnp.full((GRID_W, LANES), NEG_BIG, F32)
    cache = {}

    def pair_tile(a0, a1):
        key = (a0, a1)
        if key in cache:
            return cache[key]
        t = neg
        for d in range(n_dc):
            v0 = rpb_ref[base + a0 * n_dc + d] if a0 is not None else NEG_BIG
            v1 = rpb_ref[base + a1 * n_dc + d] if a1 is not None else NEG_BIG
            val = jnp.where(left, v0, v1)
            t = jnp.where(col_ok & (dcol == d), val, t)
        cache[key] = t
        return t

    n_pairs = (NA_KR + 1) // 2
    for i in range(NA_G):
        for jp in range(n_pairs):
            a0 = pattern[i][2 * jp]
            a1 = pattern[i][2 * jp + 1] if 2 * jp + 1 < NA_KR else None
            t = neg if (a0 is None and a1 is None) else pair_tile(a0, a1)
            width = min(LANES, NA_KR * GRID_W - jp * LANES)
            o_ref[0, i * GRID_W:(i + 1) * GRID_W, jp * LANES:jp * LANES + width] = t[:, :width]


def na_bias_tile(rpb_flat, pattern):
    tq, tk = NA_G * GRID_W, NA_KR * GRID_W
    return pl.pallas_call(
        functools.partial(_na_bias_body, pattern=pattern),
        grid=(NA_HEADS,),
        in_specs=[pl.BlockSpec(memory_space=pltpu.SMEM)],
        out_specs=pl.BlockSpec((1, tq, tk), lambda h: (h, 0, 0)),
        out_shape=jax.ShapeDtypeStruct((NA_HEADS, tq, tk), F32),
        compiler_params=_cp(("parallel",)),
        name="na_bias",
    )(rpb_flat)


def _na_attn_body(q_ref, k_ref, v_ref, bias_ref, o_ref, *, rows):
    g = pl.program_id(2)
    kb = jnp.clip(g * NA_G - NA_WIN_R // 2, 0, rows - NA_KR)
    start = pl.multiple_of(kb * GRID_W, GRID_W)
    tk = NA_KR * GRID_W
    kw = k_ref[pl.ds(start, tk), :]
    vw = v_ref[pl.ds(start, tk), :]
    q = q_ref[...]
    lane = lax.broadcasted_iota(jnp.int32, q.shape, 1)
    first = lane < NA_HEAD_DIM
    zero = jnp.zeros_like(q)
    outs = []
    for hh in range(2):
        qm = jnp.where(first if hh == 0 else jnp.logical_not(first), q, zero)
        s = _dot_nt(qm, kw) * (NA_HEAD_DIM ** -0.5) + bias_ref[0, hh]
        m = jnp.max(s, axis=-1, keepdims=True)
        p = jnp.exp(s - m)
        l = jnp.sum(p, axis=-1, keepdims=True)
        outs.append(_dot(p.astype(BF16), vw) / l)
    o_ref[...] = jnp.where(first, outs[0], outs[1]).astype(BF16)


def na_attention(P, bias, B, L):
    rows = L // GRID_W
    ng = rows // NA_G
    tq, tk = NA_G * GRID_W, NA_KR * GRID_W
    hp_n = NA_HEADS // 2
    qb, kb_, vb = COL_NAQ // LANES, COL_NAK // LANES, COL_NAV // LANES

    def bias_map(b, hp, g):
        t = jnp.where(g == 0, 0, jnp.where(g == ng - 1, 2, 1))
        return (t, hp, 0, 0)

    return pl.pallas_call(
        functools.partial(_na_attn_body, rows=rows),
        grid=(B, hp_n, ng),
        in_specs=[
            pl.BlockSpec((tq, LANES), lambda b, hp, g: (b * ng + g, qb + hp)),
            pl.BlockSpec((L, LANES), lambda b, hp, g: (b, kb_ + hp)),
            pl.BlockSpec((L, LANES), lambda b, hp, g: (b, vb + hp)),
            pl.BlockSpec((1, 2, tq, tk), bias_map),
        ],
        out_specs=pl.BlockSpec((tq, LANES), lambda b, hp, g: (b * ng + g, hp)),
        out_shape=jax.ShapeDtypeStruct((B * L, NA_HEADS * NA_HEAD_DIM), BF16),
        compiler_params=_cp(("parallel", "parallel", "arbitrary")),
        name="na_attn",
    )(P, P, P, bias)


def _mem_attn_body(q_ref, kv_ref, o_ref):
    hd = MEM_HEAD_DIM
    outs = []
    for h in range(MEM_HEADS):
        q = q_ref[:, h * hd:(h + 1) * hd]
        k = kv_ref[:, h * hd:(h + 1) * hd]
        v = kv_ref[:, (MEM_HEADS + h) * hd:(MEM_HEADS + h + 1) * hd]
        s = _dot_nt(q, k) * (hd ** -0.5)
        m = jnp.max(s, axis=-1, keepdims=True)
        p = jnp.exp(s - m)
        l = jnp.sum(p, axis=-1, keepdims=True)
        outs.append(_dot(p.astype(BF16), v) / l)
    o_ref[...] = jnp.concatenate(outs, axis=-1).astype(BF16)


def mem_attention(P, kv, B, L, n_mem, tq=1024):
    W = MEM_HEADS * MEM_HEAD_DIM
    nt = L // tq
    return pl.pallas_call(
        _mem_attn_body,
        grid=(B, nt),
        in_specs=[
            pl.BlockSpec((tq, W), lambda b, t: (b * nt + t, COL_MQ // W)),
            pl.BlockSpec((n_mem, 2 * W), lambda b, t: (b, 0)),
        ],
        out_specs=pl.BlockSpec((tq, W), lambda b, t: (b * nt + t, 0)),
        out_shape=jax.ShapeDtypeStruct((B * L, W), BF16),
        compiler_params=_cp(("parallel", "parallel")),
        name="mem_attn",
    )(P, kv)


def _log_sigmoid(x):
    return jnp.minimum(x, 0.0) - jnp.log(1.0 + jnp.exp(-jnp.abs(x)))


def _gla_body(*refs, reverse, combine):
    if combine:
        q_ref, k_ref, v_ref, lr_ref, up_ref, gb_ref, ob_ref, ng_ref, o_ref, st_ref = refs
    else:
        q_ref, k_ref, v_ref, lr_ref, up_ref, gb_ref, o_ref, st_ref = refs
    T, C = GLA_T, GLA_CHUNK
    n_sub = T // C
    HK = GLA_HEADS * GLA_DK
    HV = GLA_HEADS * GLA_DV

    @pl.when(pl.program_id(1) == 0)
    def _():
        st_ref[...] = jnp.zeros_like(st_ref)

    la = _log_sigmoid(_dot3(lr_ref[...], up_ref[...]) + gb_ref[...]) * (1.0 / GLA_TAU)
    r = lax.broadcasted_iota(jnp.int32, (T, T), 0)
    s = lax.broadcasted_iota(jnp.int32, (T, T), 1)
    same = (r // C) == (s // C)
    if reverse:
        csum_mask = same & (s >= r)
        att_mask = same & (s > r)
    else:
        csum_mask = same & (s <= r)
        att_mask = same & (s <= r)
    tri = jnp.where(csum_mask, 1.0, 0.0).astype(BF16)
    la_hi, la_lo = _split_bf16(la)
    bc = _dot(tri, la_hi) + _dot(tri, la_lo)

    edge_row = 0 if reverse else C - 1
    mid_row = C // 2 if reverse else C // 2 - 1
    edge = jnp.concatenate(
        [jnp.broadcast_to(bc[i * C + edge_row:i * C + edge_row + 1], (C, HK)) for i in range(n_sub)], axis=0)
    mid = jnp.concatenate(
        [jnp.broadcast_to(bc[i * C + mid_row:i * C + mid_row + 1], (C, HK)) for i in range(n_sub)], axis=0)

    q = q_ref[...].astype(F32) * (GLA_DK ** -0.5)
    k = k_ref[...].astype(F32)
    v = v_ref[...]
    qe = (q * jnp.exp(bc - mid)).astype(BF16)
    ke = (k * jnp.exp(mid - bc)).astype(BF16)
    qb = (q * jnp.exp(bc)).astype(BF16)
    kl = (k * jnp.exp(edge - bc)).astype(BF16)

    lane = lax.broadcasted_iota(jnp.int32, (T, HK), 1)
    intra = []
    for h in range(GLA_HEADS):
        hm = (lane // GLA_DK) == h
        att = _dot_nt(jnp.where(hm, qe, jnp.zeros_like(qe)), ke)
        att = jnp.where(att_mask, att, 0.0).astype(BF16)
        intra.append(_dot(att, v[:, h * GLA_DV:(h + 1) * GLA_DV]))
    o = jnp.concatenate(intra, axis=-1)

    er = lax.broadcasted_iota(jnp.int32, (HV, HK), 0) // GLA_DV
    dc = lax.broadcasted_iota(jnp.int32, (HV, HK), 1) // GLA_DK
    bd = er == dc
    st = st_ref[...]
    inter = [None] * n_sub
    order = range(n_sub - 1, -1, -1) if reverse else range(n_sub)
    for i in order:
        rows = slice(i * C, (i + 1) * C)
        inter[i] = _dot_nt(qb[rows], st.astype(BF16))
        u = _dot_tn(v[rows], kl[rows])
        dec = jnp.exp(edge[i * C:i * C + 1])
        st = dec * st + jnp.where(bd, u, 0.0)
    st_ref[...] = st
    o = o + jnp.concatenate(inter, axis=0)

    if combine:
        o = o + ob_ref[...]
        outs = []
        for h in range(GLA_HEADS):
            oh = o[:, h * GLA_DV:(h + 1) * GLA_DV]
            outs.append(oh * lax.rsqrt(jnp.mean(oh * oh, axis=-1, keepdims=True) + RMS_EPS))
        o_ref[...] = (jnp.concatenate(outs, axis=-1) * ng_ref[...]).astype(o_ref.dtype)
    else:
        o_ref[...] = o


def gla_direction(P, LR, up_pad, gbias, B, L, *, reverse, o_other=None, norm_g=None):
    T = GLA_T
    nb = L // T
    HK, HV = GLA_HEADS * GLA_DK, GLA_HEADS * GLA_DV
    combine = o_other is not None

    def tok(b, j):
        return b * nb + (nb - 1 - j if reverse else j)

    in_specs = [
        pl.BlockSpec((T, HK), lambda b, j: (tok(b, j), COL_GQ // HK)),
        pl.BlockSpec((T, HK), lambda b, j: (tok(b, j), COL_GK // HK)),
        pl.BlockSpec((T, HV), lambda b, j: (tok(b, j), COL_GV // HV)),
        pl.BlockSpec((T, LR_PAD), lambda b, j: (tok(b, j), 0)),
        pl.BlockSpec((LR_PAD, HK), lambda b, j: (0, 0)),
        pl.BlockSpec((1, HK), lambda b, j: (0, 0)),
    ]
    args = [P, P, P, LR, up_pad, gbias]
    if combine:
        in_specs += [pl.BlockSpec((T, HV), lambda b, j: (tok(b, j), 0)),
                     pl.BlockSpec((1, HV), lambda b, j: (0, 0))]
        args += [o_other, norm_g]
    return pl.pallas_call(
        functools.partial(_gla_body, reverse=reverse, combine=combine),
        grid=(B, nb),
        in_specs=in_specs,
        out_specs=pl.BlockSpec((T, HV), lambda b, j: (tok(b, j), 0)),
        out_shape=jax.ShapeDtypeStruct((B * L, HV), BF16 if combine else F32),
        scratch_shapes=[pltpu.VMEM((HV, HK), F32)],
        compiler_params=_cp(("parallel", "arbitrary")),
        name="gla_fwd" if combine else "gla_bwd",
    )(*args)


def _hy_conv_body(v_ref, x0_ref, x1_ref, wv_ref, wx0_ref, wx1_ref, bv_ref, bx0_ref, bx1_ref,
                  u_ref, x0o_ref, *, L, R):
    n_chunks = L // R
    HALO = 16
    rid = lax.broadcasted_iota(jnp.int32, (R, LANES), 0)

    def conv(ref, w_ref, b_ref, i):
        base = pl.multiple_of(i * R, R)
        cur = ref[pl.ds(base, R), :].astype(F32)
        pstart = pl.multiple_of(jnp.maximum(base - HALO, 0), HALO)
        nstart = pl.multiple_of(jnp.minimum(base + R, L - HALO), HALO)
        prev_row = ref[pl.ds(pstart, HALO), :][HALO - 1:HALO].astype(F32) * (i > 0).astype(F32)
        next_row = ref[pl.ds(nstart, HALO), :][0:1].astype(F32) * (i < n_chunks - 1).astype(F32)
        up = jnp.where(rid == 0, prev_row, pltpu.roll(cur, 1, 0))
        dn = jnp.where(rid == R - 1, next_row, pltpu.roll(cur, R - 1, 0))
        w = w_ref[...]
        return b_ref[...] + up * w[0:1] + cur * w[1:2] + dn * w[2:3]

    def step(i, carry):
        base = pl.multiple_of(i * R, R)
        cv = conv(v_ref, wv_ref, bv_ref, i)
        cx1 = conv(x1_ref, wx1_ref, bx1_ref, i)
        u_ref[pl.ds(base, R), :] = (cx1 * cv).astype(BF16)
        x0o_ref[pl.ds(base, R), :] = conv(x0_ref, wx0_ref, bx0_ref, i).astype(BF16)
        return carry

    lax.fori_loop(0, n_chunks, step, 0)


def hy_short_conv(P, conv_w, conv_b, B, L, R=1024):
    nc = HY_W // LANES
    bv, bx0, bx1 = COL_HYV // LANES, COL_HYX0 // LANES, COL_HYX1 // LANES
    seq = lambda off: pl.BlockSpec((L, LANES), lambda b, c: (b, off + c))
    wsp = lambda seg: pl.BlockSpec((HY_CONV, LANES), lambda b, c: (0, seg * nc + c))
    bsp = lambda seg: pl.BlockSpec((1, LANES), lambda b, c: (0, seg * nc + c))
    out = pl.BlockSpec((L, LANES), lambda b, c: (b, c))
    return pl.pallas_call(
        functools.partial(_hy_conv_body, L=L, R=min(R, L)),
        grid=(B, nc),
        in_specs=[seq(bv), seq(bx0), seq(bx1), wsp(0), wsp(1), wsp(2), bsp(0), bsp(1), bsp(2)],
        out_specs=[out, out],
        out_shape=[jax.ShapeDtypeStruct((B * L, HY_W), BF16)] * 2,
        compiler_params=_cp(("parallel", "parallel")),
        name="hy_conv",
    )(P, P, P, conv_w, conv_w, conv_w, conv_b, conv_b, conv_b)


def _hy_filter_body(freq_ref, delta_ref, w1_ref, b1_ref, wm_ref, bm_ref, w3_ref, k_ref, e_ref, *, L, R):
    i = pl.program_id(0)
    t = i * R + lax.broadcasted_iota(jnp.int32, (R, 1), 0)
    back = t >= L
    pos = jnp.where(back, 2 * L - t, t).astype(F32)
    t01 = pos / float(L - 1)
    lane = lax.broadcasted_iota(jnp.int32, (R, LANES), 1)
    ang = ((2.0 * math.pi / L) * pos) * freq_ref[...]
    z = jnp.where(lane == 0, t01,
                  jnp.where(lane <= HY_EMB_BANDS, jnp.cos(ang),
                            jnp.where(lane < HY_EMB, -jnp.sin(ang), 0.0)))
    h = jnp.sin(HY_SIN_FREQ * (_dot3(z, w1_ref[...]) + b1_ref[...]))
    for j in range(HY_INNER_MLPS):
        h = jnp.sin(HY_SIN_FREQ * (_dot3(h, wm_ref[j]) + bm_ref[j]))
    hw = _dot3(h, w3_ref[...])
    window = jnp.exp(-t01 * delta_ref[...])
    taps = jnp.where(back, hw[:, HY_W:], hw[:, :HY_W]) * window
    taps = jnp.where(t == L, 0.0, taps)
    k_ref[...] = taps.astype(BF16)

    @pl.when(i == 0)
    def _():
        e_ref[...] = jnp.zeros_like(e_ref)

    e_ref[...] += jnp.sum(taps * taps, axis=0, keepdims=True)


def hy_filter(freq, delta, w1p, b1p, wmp, bmp, w3p, L, R=1024):
    R = min(R, L)
    full = lambda a: pl.BlockSpec(a.shape, lambda i, _n=a.ndim: (0,) * _n)
    args = (freq, delta, w1p, b1p, wmp, bmp, w3p)
    return pl.pallas_call(
        functools.partial(_hy_filter_body, L=L, R=R),
        grid=(2 * L // R,),
        in_specs=[full(a) for a in args],
        out_specs=[pl.BlockSpec((R, HY_W), lambda i: (i, 0)), pl.BlockSpec((1, HY_W), lambda i: (0, 0))],
        out_shape=[jax.ShapeDtypeStruct((2 * L, HY_W), BF16), jax.ShapeDtypeStruct((1, HY_W), F32)],
        compiler_params=_cp(("arbitrary",)),
        name="hy_filter",
    )(*args)


def _dft_slow_body(c_ref, x_ref, o_ref):
    two, half, tn = x_ref.shape
    o_ref[...] = _dot(c_ref[...], x_ref[...].reshape(two * half, tn)).astype(o_ref.dtype)


def dft_slow(cmat, x4, tn=4096):
    R, K = cmat.shape
    npair, _, half, NL = x4.shape
    tn = min(tn, NL)
    return pl.pallas_call(
        _dft_slow_body,
        grid=(npair, NL // tn),
        in_specs=[
            pl.BlockSpec((R, K), lambda p, j: (0, 0)),
            pl.BlockSpec((None, 2, half, tn), lambda p, j: (p, 0, 0, j)),
        ],
        out_specs=pl.BlockSpec((None, R, tn), lambda p, j: (p, 0, j)),
        out_shape=jax.ShapeDtypeStruct((npair, R, NL), BF16),
        compiler_params=_cp(("parallel", "parallel")),
        name="dft_slow",
    )(cmat, x4)


def _stack_complex(tab_ref, k):
    cr, ci = tab_ref[k, 0], tab_ref[k, 1]
    return jnp.concatenate([jnp.concatenate([cr, -ci], axis=1), jnp.concatenate([ci, cr], axis=1)], axis=0)


def _hy_kf_body(a_ref, cf_ref, e_ref, o_ref, *, F, n_fft):
    n2 = DFT_N2
    scale = lax.rsqrt(e_ref[...] + 1e-6) * (1.0 / n_fft)
    for k in range(F):
        a = jnp.concatenate([a_ref[0, k], a_ref[1, k]], axis=0)
        x = _dot(_stack_complex(cf_ref, k), a) * scale
        o_ref[k, 0] = x[:n2]
        o_ref[k, 1] = x[n2:]


def hy_filter_spectrum(ak, cf_tab, energy, n_fft, F=8):
    n1, n2 = n_fft // DFT_N2, DFT_N2
    F = min(F, n1)
    a5 = ak.reshape(2, n1, n2, HY_W)
    return pl.pallas_call(
        functools.partial(_hy_kf_body, F=F, n_fft=n_fft),
        grid=(n1 // F,),
        in_specs=[
            pl.BlockSpec((2, F, n2, HY_W), lambda f: (0, f, 0, 0)),
            pl.BlockSpec((F, 2, n2, n2), lambda f: (f, 0, 0, 0)),
            pl.BlockSpec((1, HY_W), lambda f: (0, 0)),
        ],
        out_specs=pl.BlockSpec((F, 2, n2, HY_W), lambda f: (f, 0, 0, 0)),
        out_shape=jax.ShapeDtypeStruct((n1, 2, n2, HY_W), F32),
        compiler_params=_cp(("parallel",)),
        name="hy_kf",
    )(a5, cf_tab, energy)


def _hy_mid_body(a_ref, kf_ref, cf_ref, ci_ref, o_ref, *, F):
    n2 = DFT_N2
    for k in range(F):
        a = jnp.concatenate([a_ref[0, k], a_ref[1, k]], axis=0)
        x = _dot(_stack_complex(cf_ref, k), a)
        xr, xi = x[:n2], x[n2:]
        kr, ki = kf_ref[k, 0], kf_ref[k, 1]
        y = jnp.concatenate([xr * kr - xi * ki, xr * ki + xi * kr], axis=0).astype(BF16)
        bv = _dot(_stack_complex(ci_ref, k), y)
        o_ref[0, k] = bv[:n2].astype(BF16)
        o_ref[1, k] = bv[n2:].astype(BF16)


def hy_mid(a, kf, cf_tab, ci_tab, F=8):
    npair = a.shape[0]
    n1, n2 = kf.shape[0], DFT_N2
    F = min(F, n1)
    a5 = a.reshape(npair, 2, n1, n2, HY_W)
    out = pl.pallas_call(
        functools.partial(_hy_mid_body, F=F),
        grid=(n1 // F, npair),
        in_specs=[
            pl.BlockSpec((None, 2, F, n2, HY_W), lambda f, p: (p, 0, f, 0, 0)),
            pl.BlockSpec((F, 2, n2, HY_W), lambda f, p: (f, 0, 0, 0)),
            pl.BlockSpec((F, 2, n2, n2), lambda f, p: (f, 0, 0, 0)),
            pl.BlockSpec((F, 2, n2, n2), lambda f, p: (f, 0, 0, 0)),
        ],
        out_specs=pl.BlockSpec((None, 2, F, n2, HY_W), lambda f, p: (p, 0, f, 0, 0)),
        out_shape=jax.ShapeDtypeStruct((npair, 2, n1, n2, HY_W), BF16),
        compiler_params=_cp(("parallel", "arbitrary")),
        name="hy_mid",
    )(a5, kf, cf_tab, ci_tab)
    return out.reshape(npair, 2 * n1, n2 * HY_W)


def _hy_out_body(c_ref, b_ref, u_ref, x0_ref, skip_ref, o_ref):
    y = _dot(c_ref[...], b_ref[...])
    half = y.shape[0] // 2
    skip = skip_ref[...]
    for s in range(2):
        conv = y[s * half:(s + 1) * half] + skip * u_ref[s].astype(F32)
        o_ref[s] = (x0_ref[s].astype(F32) * conv).astype(BF16)


def hy_out(cmat, bsp, u4, x04, skip_t, tn=4096):
    npair, K, NL = bsp.shape
    half = u4.shape[2]
    tn = min(tn, NL)
    pair = pl.BlockSpec((None, 2, half, tn), lambda p, j: (p, 0, 0, j))
    return pl.pallas_call(
        _hy_out_body,
        grid=(npair, NL // tn),
        in_specs=[
            pl.BlockSpec((2 * half, K), lambda p, j: (0, 0)),
            pl.BlockSpec((None, K, tn), lambda p, j: (p, 0, j)),
            pair, pair,
            pl.BlockSpec((1, tn), lambda p, j: (0, j)),
        ],
        out_specs=pair,
        out_shape=jax.ShapeDtypeStruct((npair, 2, half, NL), BF16),
        compiler_params=_cp(("parallel", "parallel")),
        name="hy_out",
    )(cmat, bsp, u4, x04, skip_t)


@functools.lru_cache(maxsize=None)
def _dft_tables(L):
    n = 2 * L
    n2 = DFT_N2
    n1 = n // n2
    hl = n1 // 2
    f1 = np.arange(n1)
    t1 = np.arange(n1)
    th1 = 2.0 * np.pi * ((f1[:, None] * t1[None, :]) % n1) / n1
    fr, fi = np.cos(th1), -np.sin(th1)
    c1_data = np.block([[fr[:, :hl], -fi[:, :hl]], [fi[:, :hl], fr[:, :hl]]])
    c1_real = np.concatenate([fr, fi], axis=0)
    gr, gi = np.cos(th1.T)[:hl], np.sin(th1.T)[:hl]
    c1_inv = np.block([[gr, -gi], [gi, gr]])
    f2 = np.arange(n2)
    t2 = np.arange(n2)
    k = (t2[None, None, :] * (f1[:, None, None] + n1 * f2[None, :, None])) % n
    th = 2.0 * np.pi * k / n
    cf = np.stack([np.cos(th), -np.sin(th)], axis=1)
    tht = np.transpose(th, (0, 2, 1))
    ci = np.stack([np.cos(tht), np.sin(tht)], axis=1)
    return tuple(a.astype(np.float32).astype(BF16) for a in (c1_data, c1_real, c1_inv, cf, ci))


def hyena_branch(P, B, L, conv_w, conv_b, w1, b1, wm, bm, w3, skip):
    assert B % 2 == 0 and (2 * L) % DFT_N2 == 0 and L % DFT_N2 == 0
    n_fft = 2 * L
    n1, n2 = n_fft // DFT_N2, DFT_N2
    hl = n1 // 2
    NL = n2 * HY_W
    c1_data, c1_real, c1_inv, cf_tab, ci_tab = (jnp.asarray(a) for a in _dft_tables(L))

    hid = LANES
    freqs = np.linspace(1e-4, HY_EMB_BANDS - 1, HY_EMB_BANDS, dtype=np.float32)
    freq_lane = np.zeros((1, LANES), np.float32)
    freq_lane[0, 1:1 + HY_EMB_BANDS] = freqs
    freq_lane[0, 1 + HY_EMB_BANDS:HY_EMB] = freqs
    delta = np.abs(np.linspace(HY_MIN_DECAY, HY_MAX_DECAY, HY_W, dtype=np.float32)).reshape(1, HY_W)
    w1p = jnp.zeros((LANES, hid), F32).at[:HY_EMB, :HY_FILTER_HID].set(w1)
    b1p = jnp.zeros((1, hid), F32).at[0, :HY_FILTER_HID].set(b1)
    wmp = jnp.zeros((HY_INNER_MLPS, hid, hid), F32).at[:, :HY_FILTER_HID, :HY_FILTER_HID].set(wm)
    bmp = jnp.zeros((HY_INNER_MLPS, 1, hid), F32).at[:, 0, :HY_FILTER_HID].set(bm)
    w3p = jnp.zeros((hid, 2 * HY_W), F32).at[:HY_FILTER_HID].set(w3)
    kern, energy = hy_filter(jnp.asarray(freq_lane), jnp.asarray(delta), w1p, b1p, wmp, bmp, w3p, L)
    ak = dft_slow(c1_real, kern.reshape(1, 2, hl, NL))[0]
    kf = hy_filter_spectrum(ak, cf_tab, energy, n_fft)

    u, x0c = hy_short_conv(P, conv_w, conv_b.reshape(1, -1), B, L)
    u4 = u.reshape(B // 2, 2, hl, NL)
    x04 = x0c.reshape(B // 2, 2, hl, NL)
    a = dft_slow(c1_data, u4)
    bsp = hy_mid(a, kf, cf_tab, ci_tab)
    skip_t = jnp.tile(skip.reshape(1, HY_W), (1, n2))
    return hy_out(c1_inv, bsp, u4, x04, skip_t).reshape(B * L, HY_W)


def _merge_body(ona_ref, ohy_ref, ogl_ref, omem_ref, z_ref, g_ref, wbr_ref, wout_ref, bout_ref,
                x_ref, lg_ref, lb_ref, h32_ref, h16_ref, *, alpha):
    D = x_ref.shape[-1]
    merged = None
    for j, o_ref in enumerate((ona_ref, ohy_ref, ogl_ref, omem_ref)):
        z = z_ref[:, j * BRANCH_W:(j + 1) * BRANCH_W].astype(F32)
        act = (o_ref[...].astype(F32) * (z * jax.nn.sigmoid(z))).astype(BF16)
        gate = jax.nn.sigmoid(g_ref[:, j * D:(j + 1) * D].astype(F32))
        term = gate * _dot(act, wbr_ref[j])
        merged = term if merged is None else merged + term
    out = _dot(merged.astype(BF16), wout_ref[...]) + bout_ref[...]
    h = _layer_norm_rows(alpha * x_ref[...] + out, lg_ref[...], lb_ref[...])
    h32_ref[...] = h
    h16_ref[...] = h.astype(BF16)


def merge_layer(o_na, o_hy, o_gla, o_mem, P, wbr, wout, bout, x32, lg, lb, alpha, tm=256):
    M, D = x32.shape
    row = lambda w, blk=0: pl.BlockSpec((tm, w), lambda i, _b=blk: (i, _b))
    vec = pl.BlockSpec((1, D), lambda i: (0, 0))
    return pl.pallas_call(
        functools.partial(_merge_body, alpha=alpha),
        grid=(M // tm,),
        in_specs=[
            row(BRANCH_W), row(BRANCH_W), row(BRANCH_W), row(BRANCH_W),
            row(N_BRANCH * BRANCH_W, COL_Z // (N_BRANCH * BRANCH_W)),
            row(N_BRANCH * D, COL_G // (N_BRANCH * D)),
            pl.BlockSpec((N_BRANCH, BRANCH_W, D), lambda i: (0, 0, 0)),
            pl.BlockSpec((D, D), lambda i: (0, 0)),
            vec, row(D), vec, vec,
        ],
        out_specs=[row(D), row(D)],
        out_shape=[jax.ShapeDtypeStruct((M, D), F32), jax.ShapeDtypeStruct((M, D), BF16)],
        compiler_params=_cp(("parallel",)),
        name="merge",
    )(o_na, o_hy, o_gla, o_mem, P, P, wbr, wout, bout, x32, lg, lb)


def _permute_in_proj(w_in, b_in):
    sizes = (512, 512, 512, 1536, 256, 256, 512, GLA_RANK, GLA_RANK, 512, 2048, 4096)
    offs = np.concatenate([[0], np.cumsum(sizes)])
    seg = lambda a, i: a[..., offs[i]:offs[i + 1]]
    order = (11, 10, 0, 1, 2, 3, 4, 5, 6, 9)
    wm = jnp.concatenate([seg(w_in, i) for i in order], axis=-1).astype(BF16)
    bm = jnp.concatenate([seg(b_in, i) for i in order], axis=-1).reshape(1, -1)
    pad = LR_PAD - 2 * GLA_RANK
    wlr = jnp.pad(jnp.concatenate([seg(w_in, 7), seg(w_in, 8)], axis=-1), ((0, 0), (0, pad))).astype(BF16)
    blr = jnp.pad(jnp.concatenate([seg(b_in, 7), seg(b_in, 8)], axis=-1), ((0, pad),)).reshape(1, -1)
    return wm, bm, wlr, blr


def _hybrid_layer(h32, h16, mem16, B, L, depth, w_in, b_in, na_rpb, hy_conv_w, hy_conv_b, hy_w1, hy_b1,
                  hy_wm, hy_bm, hy_w3, hy_skip, gla_up_f, gla_bias_f, gla_up_b, gla_bias_b, gla_norm_g,
                  mem_w_kv, w_br, w_out, b_out, ln_g, ln_b):
    D = h32.shape[-1]
    rows = L // GRID_W
    assert rows % NA_G == 0 and rows >= NA_KR and L % GLA_T == 0
    wm, bm, wlr, blr = _permute_in_proj(w_in, b_in)
    assert wm.shape[1] == N_MAIN
    P, LR = in_proj(h16, wm, bm, wlr, blr)

    ng = rows // NA_G
    geo = [_na_group_geometry(g, rows) for g in range(ng)]
    assert all(geo[g][2] == geo[1][2] for g in range(1, ng - 1))
    rpb_flat = na_rpb.reshape(-1)
    bias = jnp.stack([na_bias_tile(rpb_flat, geo[g][2]) for g in (0, 1, ng - 1)], axis=0)
    o_na = na_attention(P, bias, B, L)

    o_hy = hyena_branch(P, B, L, hy_conv_w, hy_conv_b, hy_w1, hy_b1, hy_wm, hy_bm, hy_w3, hy_skip)

    HK = GLA_HEADS * GLA_DK
    up_f = jnp.zeros((LR_PAD, HK), F32).at[:GLA_RANK].set(gla_up_f)
    up_b = jnp.zeros((LR_PAD, HK), F32).at[GLA_RANK:2 * GLA_RANK].set(gla_up_b)
    o_b = gla_direction(P, LR, up_b, gla_bias_b.reshape(1, HK), B, L, reverse=True)
    o_gla = gla_direction(P, LR, up_f, gla_bias_f.reshape(1, HK), B, L, reverse=False,
                          o_other=o_b, norm_g=gla_norm_g.reshape(1, -1))

    n_mem = mem16.shape[0] // B
    kv = matmul_bf16(mem16, mem_w_kv.astype(BF16))
    o_mem = mem_attention(P, kv, B, L, n_mem)

    alpha = (2 * depth) ** 0.25
    return merge_layer(o_na, o_hy, o_gla, o_mem, P, w_br.astype(BF16), w_out.astype(BF16),
                       b_out.reshape(1, D), h32, ln_g.reshape(1, D), ln_b.reshape(1, D), alpha)


def kernel(x, mem, ln_in_g, ln_in_b, w_in, b_in, na_rpb, hy_conv_w, hy_conv_b, hy_w1, hy_b1, hy_wm, hy_bm,
           hy_w3, hy_skip, gla_up_f, gla_bias_f, gla_up_b, gla_bias_b, gla_norm_g, mem_w_kv, w_br, w_out,
           b_out, ln_g, ln_b):
    B, L, D = x.shape
    depth = w_in.shape[0]
    h32, h16 = ln_in(x.reshape(B * L, D), ln_in_g, ln_in_b)
    mem16 = mem.reshape(-1, D).astype(BF16)
    for i in range(depth):
        h32, h16 = _hybrid_layer(
            h32, h16, mem16, B, L, depth, w_in[i], b_in[i], na_rpb[i], hy_conv_w[i], hy_conv_b[i], hy_w1[i],
            hy_b1[i], hy_wm[i], hy_bm[i], hy_w3[i], hy_skip[i], gla_up_f[i], gla_bias_f[i], gla_up_b[i],
            gla_bias_b[i], gla_norm_g[i], mem_w_kv[i], w_br[i], w_out[i], b_out[i], ln_g[i], ln_b[i])
    return h32.reshape(B, L, D)
```

```python
import functools
import math

import numpy as np
import jax
import jax.numpy as jnp
from jax import lax
from jax.experimental import pallas as pl
from jax.experimental.pallas import tpu as pltpu

F32 = jnp.float32
BF16 = jnp.bfloat16

GRID_W = 64
NA_HEADS, NA_HEAD_DIM, NA_WIN_R, NA_WIN_C = 8, 64, 8, 16
HY_W, HY_CONV, HY_EMB_BANDS, HY_FILTER_HID, HY_INNER_MLPS = 512, 3, 16, 64, 2
HY_EMB = 1 + 2 * HY_EMB_BANDS
HY_SIN_FREQ = 1.0
HY_DECAY_TARGET, HY_FAST_DECAY_PCT, HY_SLOW_DECAY_PCT = 1e-2, 0.3, 1.5
HY_MAX_DECAY = math.log(HY_DECAY_TARGET) / HY_FAST_DECAY_PCT
HY_MIN_DECAY = math.log(HY_DECAY_TARGET) / HY_SLOW_DECAY_PCT
GLA_HEADS, GLA_DK, GLA_DV, GLA_RANK, GLA_TAU, GLA_CHUNK = 4, 64, 128, 16, 16.0, 64
MEM_HEADS, MEM_HEAD_DIM = 4, 128
N_BRANCH, BRANCH_W = 4, 512
LN_EPS, RMS_EPS = 1e-5, 1e-6

LANES = 128
VMEM_LIMIT = 52 * 1024 * 1024
NEG_BIG = -1e30

COL_G = 0
COL_Z = 4096
COL_NAQ, COL_NAK, COL_NAV = 6144, 6656, 7168
COL_HYV, COL_HYX0, COL_HYX1 = 7680, 8192, 8704
COL_GQ, COL_GK, COL_GV = 9216, 9472, 9728
COL_MQ = 10240
N_MAIN = 10752
LR_PAD = 128

LOG2E = math.log2(math.e)
NA_QSCALE = NA_HEAD_DIM ** -0.5 * LOG2E
NA_G = 4
NA_KR = NA_G + NA_WIN_R - 1
NA_GROUPS_PER_STEP = 4
GLA_T = 256
GLA_BATCH_PER_STEP = 4
DFT_N2 = 128
DFT_T2_BLOCK = 16


def _cp(sem):
    return pltpu.CompilerParams(dimension_semantics=sem, vmem_limit_bytes=VMEM_LIMIT)


def _split_bf16(a):
    hi = a.astype(BF16)
    lo = (a - hi.astype(F32)).astype(BF16)
    return hi, lo


def _dot(a, b):
    return jnp.dot(a, b, preferred_element_type=F32)


def _dot3(a, b):
    ah, al = _split_bf16(a)
    bh, bl = _split_bf16(b)
    return _dot(ah, bh) + _dot(al, bh) + _dot(ah, bl)


def _dot_nt(a, b):
    return lax.dot_general(a, b, (((1,), (1,)), ((), ())), preferred_element_type=F32)


def _dot_tn(a, b):
    return lax.dot_general(a, b, (((0,), (0,)), ((), ())), preferred_element_type=F32)


def _layer_norm_rows(y, g, b):
    mu = jnp.mean(y, axis=-1, keepdims=True)
    d = y - mu
    var = jnp.mean(d * d, axis=-1, keepdims=True)
    return d * lax.rsqrt(var + LN_EPS) * g + b


def _ln_in_body(x_ref, g_ref, b_ref, h32_ref, h16_ref):
    h = _layer_norm_rows(x_ref[...], g_ref[...], b_ref[...])
    h32_ref[...] = h
    h16_ref[...] = h.astype(BF16)


def ln_in(x2, g, b, tm=512):
    M, D = x2.shape
    row = pl.BlockSpec((tm, D), lambda i: (i, 0))
    vec = pl.BlockSpec((1, D), lambda i: (0, 0))
    return pl.pallas_call(
        _ln_in_body,
        grid=(M // tm,),
        in_specs=[row, vec, vec],
        out_specs=[row, row],
        out_shape=[jax.ShapeDtypeStruct((M, D), F32), jax.ShapeDtypeStruct((M, D), BF16)],
        compiler_params=_cp(("parallel",)),
        name="ln_in",
    )(x2, g.reshape(1, D), b.reshape(1, D))


def _in_proj_body(x_ref, w_ref, b_ref, wlr_ref, blr_ref, p_ref, lr_ref):
    x = x_ref[...]
    p_ref[...] = (_dot(x, w_ref[...]) + b_ref[...]).astype(BF16)

    @pl.when(pl.program_id(1) == 0)
    def _():
        lr_ref[...] = _dot(x, wlr_ref[...]) + blr_ref[...]


def in_proj(h16, wm, bm, wlr, blr, tm=2048, tn=1536):
    M, D = h16.shape
    NM = wm.shape[1]
    return pl.pallas_call(
        _in_proj_body,
        grid=(M // tm, NM // tn),
        in_specs=[
            pl.BlockSpec((tm, D), lambda i, j: (i, 0)),
            pl.BlockSpec((D, tn), lambda i, j: (0, j)),
            pl.BlockSpec((1, tn), lambda i, j: (0, j)),
            pl.BlockSpec((D, LR_PAD), lambda i, j: (0, 0)),
            pl.BlockSpec((1, LR_PAD), lambda i, j: (0, 0)),
        ],
        out_specs=[
            pl.BlockSpec((tm, tn), lambda i, j: (i, j)),
            pl.BlockSpec((tm, LR_PAD), lambda i, j: (i, 0)),
        ],
        out_shape=[jax.ShapeDtypeStruct((M, NM), BF16), jax.ShapeDtypeStruct((M, LR_PAD), F32)],
        compiler_params=_cp(("parallel", "arbitrary")),
        name="in_proj",
    )(h16, wm, bm, wlr, blr)


def _matmul_body(x_ref, w_ref, o_ref):
    o_ref[...] = _dot(x_ref[...], w_ref[...]).astype(o_ref.dtype)


def matmul_bf16(x, w, tm=512):
    M, K = x.shape
    N = w.shape[1]
    return pl.pallas_call(
        _matmul_body,
        grid=(M // tm,),
        in_specs=[pl.BlockSpec((tm, K), lambda i: (i, 0)), pl.BlockSpec((K, N), lambda i: (0, 0))],
        out_specs=pl.BlockSpec((tm, N), lambda i: (i, 0)),
        out_shape=jax.ShapeDtypeStruct((M, N), BF16),
        compiler_params=_cp(("parallel",)),
        name="mem_kv",
    )(x, w)


def _na_row_start(r, rows):
    return min(max(r - NA_WIN_R // 2, 0), rows - NA_WIN_R)


def _na_group_geometry(g, rows):
    r0 = g * NA_G
    kb = min(max(r0 - NA_WIN_R // 2, 0), rows - NA_KR)
    pat = []
    for i in range(NA_G):
        rs = _na_row_start(r0 + i, rows)
        pat.append(tuple((kb + j - (r0 + i) + NA_WIN_R - 1) if rs <= kb + j < rs + NA_WIN_R else None
                         for j in range(NA_KR)))
    return r0, kb, tuple(pat)


def _na_bias_body(rpb_ref, o_ref, *, pattern):
    h = pl.program_id(0)
    n_dc = 2 * NA_WIN_C - 1
    base = h * ((2 * NA_WIN_R - 1) * n_dc)
    cc = lax.broadcasted_iota(jnp.int32, (GRID_W, LANES), 0)
    lane = lax.broadcasted_iota(jnp.int32, (GRID_W, LANES), 1)
    c = lane % GRID_W
    left = lane < GRID_W
    cs = jnp.clip(c - NA_WIN_C // 2, 0, GRID_W - NA_WIN_C)
    col_ok = (cc >= cs) & (cc < cs + NA_WIN_C)
    dcol = cc - c + NA_WIN_C - 1
    neg = jnp.full((GRID_W, LANES), NEG_BIG, F32)
    cache = {}

    def pair_tile(a0, a1):
        key = (a0, a1)
        if key in cache:
            return cache[key]
        t = neg
        for d in range(n_dc):
            v0 = rpb_ref[base + a0 * n_dc + d] * LOG2E if a0 is not None else NEG_BIG
            v1 = rpb_ref[base + a1 * n_dc + d] * LOG2E if a1 is not None else NEG_BIG
            val = jnp.where(left, v0, v1)
            t = jnp.where(col_ok & (dcol == d), val, t)
        cache[key] = t
        return t

    assert NA_G % 2 == 0
    for j in range(NA_KR):
        for ip in range(NA_G // 2):
            a0, a1 = pattern[2 * ip][j], pattern[2 * ip + 1][j]
            t = neg if (a0 is None and a1 is None) else pair_tile(a0, a1)
            o_ref[0, j * GRID_W:(j + 1) * GRID_W, ip * LANES:(ip + 1) * LANES] = t


def na_bias_tile(rpb_flat, pattern):
    tq, tk = NA_G * GRID_W, NA_KR * GRID_W
    return pl.pallas_call(
        functools.partial(_na_bias_body, pattern=pattern),
        grid=(NA_HEADS,),
        in_specs=[pl.BlockSpec(memory_space=pltpu.SMEM)],
        out_specs=pl.BlockSpec((1, tk, tq), lambda h: (h, 0, 0)),
        out_shape=jax.ShapeDtypeStruct((NA_HEADS, tk, tq), F32),
        compiler_params=_cp(("parallel",)),
        name="na_bias",
    )(rpb_flat)


def _na_attn_body(q_ref, k_ref, v_ref, *rest, rows):
    bias_refs, o_ref = rest[:-1], rest[-1]
    n_grp = len(bias_refs)
    tq, tk = NA_G * GRID_W, NA_KR * GRID_W
    g0 = pl.program_id(2) * n_grp
    kws, vws, qms, biases = [], [], [], []
    for a in range(n_grp):
        kb = jnp.clip((g0 + a) * NA_G - NA_WIN_R // 2, 0, rows - NA_KR)
        start = pl.multiple_of(kb * GRID_W, GRID_W)
        kw = k_ref[pl.ds(start, tk), :]
        vw = v_ref[pl.ds(start, tk), :]
        q = q_ref[a * tq:(a + 1) * tq, :]
        first = lax.broadcasted_iota(jnp.int32, q.shape, 1) < NA_HEAD_DIM
        zero = jnp.zeros_like(q)
        for hh in range(2):
            kws.append(kw)
            vws.append(vw)
            qms.append(jnp.where(first if hh == 0 else jnp.logical_not(first), q, zero))
            biases.append(bias_refs[a][0, hh])
    sts = [_dot_nt(kw, qm) + b for kw, qm, b in zip(kws, qms, biases)]
    ms = [jnp.max(st, axis=0, keepdims=True) for st in sts]
    ps = [jnp.exp2(st - m) for st, m in zip(sts, ms)]
    ls = [jnp.sum(p, axis=0, keepdims=True) for p in ps]
    outs = [_dot_tn(vw, p.astype(BF16)) / l for vw, p, l in zip(vws, ps, ls)]
    top = lax.broadcasted_iota(jnp.int32, outs[0].shape, 0) < NA_HEAD_DIM
    for a in range(n_grp):
        o_ref[a * tq:(a + 1) * tq, :] = jnp.where(top, outs[2 * a], outs[2 * a + 1]).T.astype(BF16)


def na_attention(P, bias, B, L):
    rows = L // GRID_W
    ng = rows // NA_G
    n_grp = NA_GROUPS_PER_STEP if ng % NA_GROUPS_PER_STEP == 0 else 1
    ns = ng // n_grp
    tq, tk = NA_G * GRID_W, NA_KR * GRID_W
    hp_n = NA_HEADS // 2
    qb, kb_, vb = COL_NAQ // LANES, COL_NAK // LANES, COL_NAV // LANES

    def bias_spec(a):
        def bias_map(b, hp, s):
            g = s * n_grp + a
            return (jnp.where(g == 0, 0, jnp.where(g == ng - 1, 2, 1)), hp, 0, 0)
        return pl.BlockSpec((1, 2, tk, tq), bias_map)

    return pl.pallas_call(
        functools.partial(_na_attn_body, rows=rows),
        grid=(B, hp_n, ns),
        in_specs=[
            pl.BlockSpec((n_grp * tq, LANES), lambda b, hp, s: (b * ns + s, qb + hp)),
            pl.BlockSpec((L, LANES), lambda b, hp, s: (b, kb_ + hp)),
            pl.BlockSpec((L, LANES), lambda b, hp, s: (b, vb + hp)),
        ] + [bias_spec(a) for a in range(n_grp)],
        out_specs=pl.BlockSpec((n_grp * tq, LANES), lambda b, hp, s: (b * ns + s, hp)),
        out_shape=jax.ShapeDtypeStruct((B * L, NA_HEADS * NA_HEAD_DIM), BF16),
        compiler_params=_cp(("parallel", "parallel", "arbitrary")),
        name="na_attn",
    )(P, P, P, *([bias] * n_grp))


def _mem_attn_body(q_ref, kv_ref, o_ref):
    hd = MEM_HEAD_DIM
    outs = []
    for h in range(MEM_HEADS):
        q = q_ref[:, h * hd:(h + 1) * hd]
        k = kv_ref[:, h * hd:(h + 1) * hd]
        v = kv_ref[:, (MEM_HEADS + h) * hd:(MEM_HEADS + h + 1) * hd]
        s = _dot_nt(q, k) * (hd ** -0.5)
        m = jnp.max(s, axis=-1, keepdims=True)
        p = jnp.exp(s - m)
        l = jnp.sum(p, axis=-1, keepdims=True)
        outs.append(_dot(p.astype(BF16), v) / l)
    o_ref[...] = jnp.concatenate(outs, axis=-1).astype(BF16)


def mem_attention(P, kv, B, L, n_mem, tq=1024):
    W = MEM_HEADS * MEM_HEAD_DIM
    nt = L // tq
    return pl.pallas_call(
        _mem_attn_body,
        grid=(B, nt),
        in_specs=[
            pl.BlockSpec((tq, W), lambda b, t: (b * nt + t, COL_MQ // W)),
            pl.BlockSpec((n_mem, 2 * W), lambda b, t: (b, 0)),
        ],
        out_specs=pl.BlockSpec((tq, W), lambda b, t: (b * nt + t, 0)),
        out_shape=jax.ShapeDtypeStruct((B * L, W), BF16),
        compiler_params=_cp(("parallel", "parallel")),
        name="mem_attn",
    )(P, kv)


def _log_sigmoid(x):
    return jnp.minimum(x, 0.0) - jnp.log(1.0 + jnp.exp(-jnp.abs(x)))


def _gla_body(*refs, reverse, combine):
    if combine:
        q_ref, k_ref, v_ref, lr_ref, up_ref, gb_ref, ob_ref, ng_ref, o_ref, st_ref = refs
    else:
        q_ref, k_ref, v_ref, lr_ref, up_ref, gb_ref, o_ref, st_ref = refs
    T, C = GLA_T, GLA_CHUNK
    n_sub = T // C
    HK = GLA_HEADS * GLA_DK
    HV = GLA_HEADS * GLA_DV

    @pl.when(pl.program_id(1) == 0)
    def _():
        st_ref[...] = jnp.zeros_like(st_ref)

    r = lax.broadcasted_iota(jnp.int32, (T, T), 0)
    s = lax.broadcasted_iota(jnp.int32, (T, T), 1)
    same = (r // C) == (s // C)
    if reverse:
        csum_mask = same & (s >= r)
        att_mask = same & (s > r)
    else:
        csum_mask = same & (s <= r)
        att_mask = same & (s <= r)
    tri = jnp.where(csum_mask, 1.0, 0.0).astype(BF16)
    lane = lax.broadcasted_iota(jnp.int32, (T, HK), 1)
    er = lax.broadcasted_iota(jnp.int32, (HV, HK), 0) // GLA_DV
    dc = lax.broadcasted_iota(jnp.int32, (HV, HK), 1) // GLA_DK
    bd = er == dc
    edge_row = 0 if reverse else C - 1
    mid_row = C // 2 if reverse else C // 2 - 1

    nbb = q_ref.shape[0]
    bbs = range(nbb)
    heads = range(GLA_HEADS)
    hms = [(lane // GLA_DK) == h for h in heads]
    las = [_log_sigmoid(_dot3(lr_ref[bb], up_ref[...]) + gb_ref[...]) * (1.0 / GLA_TAU) for bb in bbs]
    splits = [_split_bf16(la) for la in las]
    bcs = [_dot(tri, hi) + _dot(tri, lo) for hi, lo in splits]
    edges = [jnp.concatenate([jnp.broadcast_to(bc[i * C + edge_row:i * C + edge_row + 1], (C, HK))
                              for i in range(n_sub)], axis=0) for bc in bcs]
    mids = [jnp.concatenate([jnp.broadcast_to(bc[i * C + mid_row:i * C + mid_row + 1], (C, HK))
                             for i in range(n_sub)], axis=0) for bc in bcs]
    qs = [q_ref[bb].astype(F32) * (GLA_DK ** -0.5) for bb in bbs]
    ks = [k_ref[bb].astype(F32) for bb in bbs]
    vs = [v_ref[bb] for bb in bbs]
    qes = [(q * jnp.exp(bc - mid)).astype(BF16) for q, bc, mid in zip(qs, bcs, mids)]
    kes = [(k * jnp.exp(mid - bc)).astype(BF16) for k, bc, mid in zip(ks, bcs, mids)]
    atts = [[_dot_nt(jnp.where(hms[h], qes[bb], jnp.zeros_like(qes[bb])), kes[bb]) for h in heads] for bb in bbs]
    qbs = [(q * jnp.exp(bc)).astype(BF16) for q, bc in zip(qs, bcs)]
    kls = [(k * jnp.exp(edge - bc)).astype(BF16) for k, bc, edge in zip(ks, bcs, edges)]
    decs = [[jnp.exp(edge[i * C:i * C + 1]) for i in range(n_sub)] for edge in edges]
    atts = [[jnp.where(att_mask, atts[bb][h], 0.0).astype(BF16) for h in heads] for bb in bbs]
    us = [[_dot_tn(vs[bb][i * C:(i + 1) * C], kls[bb][i * C:(i + 1) * C]) for i in range(n_sub)] for bb in bbs]
    intras = [[_dot(atts[bb][h], vs[bb][:, h * GLA_DV:(h + 1) * GLA_DV]) for h in heads] for bb in bbs]

    sts = [st_ref[bb] for bb in bbs]
    inters = [[None] * n_sub for _ in bbs]
    order = range(n_sub - 1, -1, -1) if reverse else range(n_sub)
    for i in order:
        for bb in bbs:
            inters[bb][i] = _dot_nt(qbs[bb][i * C:(i + 1) * C], sts[bb].astype(BF16))
            sts[bb] = decs[bb][i] * sts[bb] + jnp.where(bd, us[bb][i], 0.0)
    for bb in bbs:
        st_ref[bb] = sts[bb]
        o = jnp.concatenate(intras[bb], axis=-1) + jnp.concatenate(inters[bb], axis=0)
        if combine:
            o = o + ob_ref[bb]
            outs = []
            for h in heads:
                oh = o[:, h * GLA_DV:(h + 1) * GLA_DV]
                outs.append(oh * lax.rsqrt(jnp.mean(oh * oh, axis=-1, keepdims=True) + RMS_EPS))
            o_ref[bb] = (jnp.concatenate(outs, axis=-1) * ng_ref[...]).astype(o_ref.dtype)
        else:
            o_ref[bb] = o


def gla_direction(P, LR, up_pad, gbias, B, L, *, reverse, o_other=None, norm_g=None):
    T = GLA_T
    nb = L // T
    nbb = GLA_BATCH_PER_STEP if B % GLA_BATCH_PER_STEP == 0 else 1
    HK, HV = GLA_HEADS * GLA_DK, GLA_HEADS * GLA_DV
    combine = o_other is not None
    P3 = P.reshape(B, L, P.shape[-1])
    LR3 = LR.reshape(B, L, LR_PAD)

    def tok(j):
        return nb - 1 - j if reverse else j

    seq = lambda w, col: pl.BlockSpec((nbb, T, w), lambda b, j: (b, tok(j), col))
    in_specs = [
        seq(HK, COL_GQ // HK), seq(HK, COL_GK // HK), seq(HV, COL_GV // HV), seq(LR_PAD, 0),
        pl.BlockSpec((LR_PAD, HK), lambda b, j: (0, 0)),
        pl.BlockSpec((1, HK), lambda b, j: (0, 0)),
    ]
    args = [P3, P3, P3, LR3, up_pad, gbias]
    if combine:
        in_specs += [seq(HV, 0), pl.BlockSpec((1, HV), lambda b, j: (0, 0))]
        args += [o_other.reshape(B, L, HV), norm_g]
    out = pl.pallas_call(
        functools.partial(_gla_body, reverse=reverse, combine=combine),
        grid=(B // nbb, nb),
        in_specs=in_specs,
        out_specs=seq(HV, 0),
        out_shape=jax.ShapeDtypeStruct((B, L, HV), BF16 if combine else F32),
        scratch_shapes=[pltpu.VMEM((nbb, HV, HK), F32)],
        compiler_params=_cp(("parallel", "arbitrary")),
        name="gla_fwd" if combine else "gla_bwd",
    )(*args)
    return out.reshape(B * L, HV)


def _hy_conv_body(v_ref, x0_ref, x1_ref, wv_ref, wx0_ref, wx1_ref, bv_ref, bx0_ref, bx1_ref,
                  u_ref, x0o_ref, *, L, R):
    n_chunks = L // R
    HALO = 16
    rid = lax.broadcasted_iota(jnp.int32, (R, LANES), 0)

    def conv(ref, w_ref, b_ref, i):
        base = pl.multiple_of(i * R, R)
        cur = ref[pl.ds(base, R), :].astype(F32)
        pstart = pl.multiple_of(jnp.maximum(base - HALO, 0), HALO)
        nstart = pl.multiple_of(jnp.minimum(base + R, L - HALO), HALO)
        prev_row = ref[pl.ds(pstart, HALO), :][HALO - 1:HALO].astype(F32) * (i > 0).astype(F32)
        next_row = ref[pl.ds(nstart, HALO), :][0:1].astype(F32) * (i < n_chunks - 1).astype(F32)
        up = jnp.where(rid == 0, prev_row, pltpu.roll(cur, 1, 0))
        dn = jnp.where(rid == R - 1, next_row, pltpu.roll(cur, R - 1, 0))
        w = w_ref[...]
        return b_ref[...] + up * w[0:1] + cur * w[1:2] + dn * w[2:3]

    def step(i, carry):
        base = pl.multiple_of(i * R, R)
        cv = conv(v_ref, wv_ref, bv_ref, i)
        cx1 = conv(x1_ref, wx1_ref, bx1_ref, i)
        u_ref[pl.ds(base, R), :] = (cx1 * cv).astype(BF16)
        x0o_ref[pl.ds(base, R), :] = conv(x0_ref, wx0_ref, bx0_ref, i).astype(BF16)
        return carry

    lax.fori_loop(0, n_chunks, step, 0)


def hy_short_conv(P, conv_w, conv_b, B, L, R=1024):
    nc = HY_W // LANES
    bv, bx0, bx1 = COL_HYV // LANES, COL_HYX0 // LANES, COL_HYX1 // LANES
    seq = lambda off: pl.BlockSpec((L, LANES), lambda b, c: (b, off + c))
    wsp = lambda seg: pl.BlockSpec((HY_CONV, LANES), lambda b, c: (0, seg * nc + c))
    bsp = lambda seg: pl.BlockSpec((1, LANES), lambda b, c: (0, seg * nc + c))
    out = pl.BlockSpec((L, LANES), lambda b, c: (b, c))
    return pl.pallas_call(
        functools.partial(_hy_conv_body, L=L, R=min(R, L)),
        grid=(B, nc),
        in_specs=[seq(bv), seq(bx0), seq(bx1), wsp(0), wsp(1), wsp(2), bsp(0), bsp(1), bsp(2)],
        out_specs=[out, out],
        out_shape=[jax.ShapeDtypeStruct((B * L, HY_W), BF16)] * 2,
        compiler_params=_cp(("parallel", "parallel")),
        name="hy_conv",
    )(P, P, P, conv_w, conv_w, conv_w, conv_b, conv_b, conv_b)


def _hy_filter_body(freq_ref, delta_ref, w1_ref, b1_ref, wm_ref, bm_ref, w3_ref, k_ref, e_ref, *, L, R):
    i = pl.program_id(0)
    t = i * R + lax.broadcasted_iota(jnp.int32, (R, 1), 0)
    back = t >= L
    pos = jnp.where(back, 2 * L - t, t).astype(F32)
    t01 = pos / float(L - 1)
    lane = lax.broadcasted_iota(jnp.int32, (R, LANES), 1)
    ang = ((2.0 * math.pi / L) * pos) * freq_ref[...]
    z = jnp.where(lane == 0, t01,
                  jnp.where(lane <= HY_EMB_BANDS, jnp.cos(ang),
                            jnp.where(lane < HY_EMB, -jnp.sin(ang), 0.0)))
    h = jnp.sin(HY_SIN_FREQ * (_dot3(z, w1_ref[...]) + b1_ref[...]))
    for j in range(HY_INNER_MLPS):
        h = jnp.sin(HY_SIN_FREQ * (_dot3(h, wm_ref[j]) + bm_ref[j]))
    hw = _dot3(h, w3_ref[...])
    window = jnp.exp(-t01 * delta_ref[...])
    taps = jnp.where(back, hw[:, HY_W:], hw[:, :HY_W]) * window
    taps = jnp.where(t == L, 0.0, taps)
    k_ref[...] = taps.astype(BF16)

    @pl.when(i == 0)
    def _():
        e_ref[...] = jnp.zeros_like(e_ref)

    e_ref[...] += jnp.sum(taps * taps, axis=0, keepdims=True)


def hy_filter(freq, delta, w1p, b1p, wmp, bmp, w3p, L, R=1024):
    R = min(R, L)
    full = lambda a: pl.BlockSpec(a.shape, lambda i, _n=a.ndim: (0,) * _n)
    args = (freq, delta, w1p, b1p, wmp, bmp, w3p)
    return pl.pallas_call(
        functools.partial(_hy_filter_body, L=L, R=R),
        grid=(2 * L // R,),
        in_specs=[full(a) for a in args],
        out_specs=[pl.BlockSpec((R, HY_W), lambda i: (i, 0)), pl.BlockSpec((1, HY_W), lambda i: (0, 0))],
        out_shape=[jax.ShapeDtypeStruct((2 * L, HY_W), BF16), jax.ShapeDtypeStruct((1, HY_W), F32)],
        compiler_params=_cp(("arbitrary",)),
        name="hy_filter",
    )(*args)


def _dft_slow_body(c_ref, x_ref, o_ref):
    two, half, t2n, w = x_ref.shape
    x = jnp.swapaxes(x_ref[...].reshape(two * half, t2n, w), 0, 1)
    c = c_ref[...]
    y = jnp.stack([_dot(c, x[t]).astype(BF16) for t in range(t2n)], axis=0)
    o_ref[...] = jnp.swapaxes(y, 0, 1).reshape(o_ref.shape)


def dft_slow(cmat, x5, t2n=DFT_T2_BLOCK):
    R, K = cmat.shape
    npair, _, half, n2, w = x5.shape
    assert K == 2 * half and n2 % t2n == 0
    return pl.pallas_call(
        _dft_slow_body,
        grid=(npair, n2 // t2n),
        in_specs=[
            pl.BlockSpec((R, K), lambda p, j: (0, 0)),
            pl.BlockSpec((None, 2, half, t2n, w), lambda p, j: (p, 0, 0, j, 0)),
        ],
        out_specs=pl.BlockSpec((None, 2, R // 2, t2n, w), lambda p, j: (p, 0, 0, j, 0)),
        out_shape=jax.ShapeDtypeStruct((npair, 2, R // 2, n2, w), BF16),
        compiler_params=_cp(("parallel", "parallel")),
        name="dft_slow",
    )(cmat, x5)


def _stack_complex(tab_ref, k):
    cr, ci = tab_ref[k, 0], tab_ref[k, 1]
    return jnp.concatenate([jnp.concatenate([cr, -ci], axis=1), jnp.concatenate([ci, cr], axis=1)], axis=0)


def _hy_kf_body(a_ref, cf_ref, e_ref, o_ref, *, F, n_fft):
    n2 = DFT_N2
    scale = lax.rsqrt(e_ref[...] + 1e-6) * (1.0 / n_fft)
    for k in range(F):
        a = jnp.concatenate([a_ref[0, k], a_ref[1, k]], axis=0)
        x = _dot(_stack_complex(cf_ref, k), a) * scale
        o_ref[k, 0] = x[:n2]
        o_ref[k, 1] = x[n2:]


def hy_filter_spectrum(a5, cf_tab, energy, n_fft, F=8):
    n1, n2 = n_fft // DFT_N2, DFT_N2
    F = min(F, n1)
    return pl.pallas_call(
        functools.partial(_hy_kf_body, F=F, n_fft=n_fft),
        grid=(n1 // F,),
        in_specs=[
            pl.BlockSpec((2, F, n2, HY_W), lambda f: (0, f, 0, 0)),
            pl.BlockSpec((F, 2, n2, n2), lambda f: (f, 0, 0, 0)),
            pl.BlockSpec((1, HY_W), lambda f: (0, 0)),
        ],
        out_specs=pl.BlockSpec((F, 2, n2, HY_W), lambda f: (f, 0, 0, 0)),
        out_shape=jax.ShapeDtypeStruct((n1, 2, n2, HY_W), F32),
        compiler_params=_cp(("parallel",)),
        name="hy_kf",
    )(a5, cf_tab, energy)


def _hy_mid_body(a_ref, kf_ref, cf_ref, ci_ref, o_ref, *, F):
    n2 = DFT_N2
    for k in range(F):
        a = jnp.concatenate([a_ref[0, k], a_ref[1, k]], axis=0)
        x = _dot(_stack_complex(cf_ref, k), a)
        xr, xi = x[:n2], x[n2:]
        kr, ki = kf_ref[k, 0], kf_ref[k, 1]
        y = jnp.concatenate([xr * kr - xi * ki, xr * ki + xi * kr], axis=0).astype(BF16)
        bv = _dot(_stack_complex(ci_ref, k), y)
        o_ref[0, k] = bv[:n2].astype(BF16)
        o_ref[1, k] = bv[n2:].astype(BF16)


def hy_mid(a5, kf, cf_tab, ci_tab, F=8):
    npair = a5.shape[0]
    n1, n2 = kf.shape[0], DFT_N2
    F = min(F, n1)
    return pl.pallas_call(
        functools.partial(_hy_mid_body, F=F),
        grid=(n1 // F, npair),
        in_specs=[
            pl.BlockSpec((None, 2, F, n2, HY_W), lambda f, p: (p, 0, f, 0, 0)),
            pl.BlockSpec((F, 2, n2, HY_W), lambda f, p: (f, 0, 0, 0)),
            pl.BlockSpec((F, 2, n2, n2), lambda f, p: (f, 0, 0, 0)),
            pl.BlockSpec((F, 2, n2, n2), lambda f, p: (f, 0, 0, 0)),
        ],
        out_specs=pl.BlockSpec((None, 2, F, n2, HY_W), lambda f, p: (p, 0, f, 0, 0)),
        out_shape=jax.ShapeDtypeStruct((npair, 2, n1, n2, HY_W), BF16),
        compiler_params=_cp(("parallel", "arbitrary")),
        name="hy_mid",
    )(a5, kf, cf_tab, ci_tab)


def _hy_out_body(c_ref, b_ref, u_ref, x0_ref, skip_ref, o_ref):
    two, n1, t2n, w = b_ref.shape
    b = jnp.swapaxes(b_ref[...].reshape(two * n1, t2n, w), 0, 1)
    c = c_ref[...]
    y = jnp.stack([_dot(c, b[t]) for t in range(t2n)], axis=0)
    y = jnp.swapaxes(y, 0, 1).reshape(o_ref.shape)
    conv = y + skip_ref[...] * u_ref[...].astype(F32)
    o_ref[...] = (x0_ref[...].astype(F32) * conv).astype(BF16)


def hy_out(cmat, bsp, u5, x05, skip, t2n=DFT_T2_BLOCK):
    npair, _, n1, n2, w = bsp.shape
    half = u5.shape[2]
    pair = pl.BlockSpec((None, 2, half, t2n, w), lambda p, j: (p, 0, 0, j, 0))
    return pl.pallas_call(
        _hy_out_body,
        grid=(npair, n2 // t2n),
        in_specs=[
            pl.BlockSpec((2 * half, 2 * n1), lambda p, j: (0, 0)),
            pl.BlockSpec((None, 2, n1, t2n, w), lambda p, j: (p, 0, 0, j, 0)),
            pair, pair,
            pl.BlockSpec((1, w), lambda p, j: (0, 0)),
        ],
        out_specs=pair,
        out_shape=jax.ShapeDtypeStruct((npair, 2, half, n2, w), BF16),
        compiler_params=_cp(("parallel", "parallel")),
        name="hy_out",
    )(cmat, bsp, u5, x05, skip)


@functools.lru_cache(maxsize=None)
def _dft_tables(L):
    n = 2 * L
    n2 = DFT_N2
    n1 = n // n2
    hl = n1 // 2
    f1 = np.arange(n1)
    t1 = np.arange(n1)
    th1 = 2.0 * np.pi * ((f1[:, None] * t1[None, :]) % n1) / n1
    fr, fi = np.cos(th1), -np.sin(th1)
    c1_data = np.block([[fr[:, :hl], -fi[:, :hl]], [fi[:, :hl], fr[:, :hl]]])
    c1_real = np.concatenate([fr, fi], axis=0)
    gr, gi = np.cos(th1.T)[:hl], np.sin(th1.T)[:hl]
    c1_inv = np.block([[gr, -gi], [gi, gr]])
    f2 = np.arange(n2)
    t2 = np.arange(n2)
    k = (t2[None, None, :] * (f1[:, None, None] + n1 * f2[None, :, None])) % n
    th = 2.0 * np.pi * k / n
    cf = np.stack([np.cos(th), -np.sin(th)], axis=1)
    tht = np.transpose(th, (0, 2, 1))
    ci = np.stack([np.cos(tht), np.sin(tht)], axis=1)
    return tuple(a.astype(np.float32).astype(BF16) for a in (c1_data, c1_real, c1_inv, cf, ci))


def hyena_branch(P, B, L, conv_w, conv_b, w1, b1, wm, bm, w3, skip):
    assert B % 2 == 0 and (2 * L) % DFT_N2 == 0 and L % DFT_N2 == 0
    n_fft = 2 * L
    n1, n2 = n_fft // DFT_N2, DFT_N2
    hl = n1 // 2
    NL = n2 * HY_W
    c1_data, c1_real, c1_inv, cf_tab, ci_tab = (jnp.asarray(a) for a in _dft_tables(L))

    hid = LANES
    freqs = np.linspace(1e-4, HY_EMB_BANDS - 1, HY_EMB_BANDS, dtype=np.float32)
    freq_lane = np.zeros((1, LANES), np.float32)
    freq_lane[0, 1:1 + HY_EMB_BANDS] = freqs
    freq_lane[0, 1 + HY_EMB_BANDS:HY_EMB] = freqs
    delta = np.abs(np.linspace(HY_MIN_DECAY, HY_MAX_DECAY, HY_W, dtype=np.float32)).reshape(1, HY_W)
    w1p = jnp.zeros((LANES, hid), F32).at[:HY_EMB, :HY_FILTER_HID].set(w1)
    b1p = jnp.zeros((1, hid), F32).at[0, :HY_FILTER_HID].set(b1)
    wmp = jnp.zeros((HY_INNER_MLPS, hid, hid), F32).at[:, :HY_FILTER_HID, :HY_FILTER_HID].set(wm)
    bmp = jnp.zeros((HY_INNER_MLPS, 1, hid), F32).at[:, 0, :HY_FILTER_HID].set(bm)
    w3p = jnp.zeros((hid, 2 * HY_W), F32).at[:HY_FILTER_HID].set(w3)
    kern, energy = hy_filter(jnp.asarray(freq_lane), jnp.asarray(delta), w1p, b1p, wmp, bmp, w3p, L)
    ak = dft_slow(c1_real, kern.reshape(1, 2, hl, n2, HY_W))[0]
    kf = hy_filter_spectrum(ak, cf_tab, energy, n_fft)

    u, x0c = hy_short_conv(P, conv_w, conv_b.reshape(1, -1), B, L)
    u5 = u.reshape(B // 2, 2, hl, n2, HY_W)
    x05 = x0c.reshape(B // 2, 2, hl, n2, HY_W)
    a = dft_slow(c1_data, u5)
    bsp = hy_mid(a, kf, cf_tab, ci_tab)
    return hy_out(c1_inv, bsp, u5, x05, skip.reshape(1, HY_W)).reshape(B * L, HY_W)


def _merge_body(ona_ref, ohy_ref, ogl_ref, omem_ref, z_ref, g_ref, wbr_ref, wout_ref, bout_ref,
                x_ref, lg_ref, lb_ref, h32_ref, h16_ref, *, alpha):
    D = x_ref.shape[-1]
    branches = range(N_BRANCH)
    o_refs = (ona_ref, ohy_ref, ogl_ref, omem_ref)
    zs = [z_ref[:, j * BRANCH_W:(j + 1) * BRANCH_W].astype(F32) for j in branches]
    acts = [(o_refs[j][...].astype(F32) * (zs[j] * (1.0 + jnp.tanh(zs[j])))).astype(BF16) for j in branches]
    projs = [_dot(acts[j], wbr_ref[j]) for j in branches]
    gates = [1.0 + jnp.tanh(g_ref[:, j * D:(j + 1) * D].astype(F32)) for j in branches]
    merged = gates[0] * projs[0]
    for j in branches[1:]:
        merged = merged + gates[j] * projs[j]
    out = _dot(merged.astype(BF16), wout_ref[...]) + bout_ref[...]
    h = _layer_norm_rows(alpha * x_ref[...] + out, lg_ref[...], lb_ref[...])
    h32_ref[...] = h
    h16_ref[...] = h.astype(BF16)


def merge_layer(o_na, o_hy, o_gla, o_mem, P, wbr, wout, bout, x32, lg, lb, alpha, tm=512):
    M, D = x32.shape
    row = lambda w, blk=0: pl.BlockSpec((tm, w), lambda i, _b=blk: (i, _b))
    vec = pl.BlockSpec((1, D), lambda i: (0, 0))
    return pl.pallas_call(
        functools.partial(_merge_body, alpha=alpha),
        grid=(M // tm,),
        in_specs=[
            row(BRANCH_W), row(BRANCH_W), row(BRANCH_W), row(BRANCH_W),
            row(N_BRANCH * BRANCH_W, COL_Z // (N_BRANCH * BRANCH_W)),
            row(N_BRANCH * D, COL_G // (N_BRANCH * D)),
            pl.BlockSpec((N_BRANCH, BRANCH_W, D), lambda i: (0, 0, 0), pipeline_mode=pl.Buffered(1)),
            pl.BlockSpec((D, D), lambda i: (0, 0), pipeline_mode=pl.Buffered(1)),
            vec, row(D), vec, vec,
        ],
        out_specs=[row(D), row(D)],
        out_shape=[jax.ShapeDtypeStruct((M, D), F32), jax.ShapeDtypeStruct((M, D), BF16)],
        compiler_params=_cp(("parallel",)),
        name="merge",
    )(o_na, o_hy, o_gla, o_mem, P, P, wbr, wout, bout, x32, lg, lb)


def _permute_in_proj(w_in, b_in):
    sizes = (512, 512, 512, 1536, 256, 256, 512, GLA_RANK, GLA_RANK, 512, 2048, 4096)
    offs = np.concatenate([[0], np.cumsum(sizes)])
    seg = lambda a, i: a[..., offs[i]:offs[i + 1]]
    order = (11, 10, 0, 1, 2, 3, 4, 5, 6, 9)
    col_scale = {0: NA_QSCALE, 10: 0.5, 11: 0.5}
    scaled = lambda a, i: seg(a, i) * col_scale[i] if i in col_scale else seg(a, i)
    wm = jnp.concatenate([scaled(w_in, i) for i in order], axis=-1).astype(BF16)
    bm = jnp.concatenate([scaled(b_in, i) for i in order], axis=-1).reshape(1, -1)
    pad = LR_PAD - 2 * GLA_RANK
    wlr = jnp.pad(jnp.concatenate([seg(w_in, 7), seg(w_in, 8)], axis=-1), ((0, 0), (0, pad))).astype(BF16)
    blr = jnp.pad(jnp.concatenate([seg(b_in, 7), seg(b_in, 8)], axis=-1), ((0, pad),)).reshape(1, -1)
    return wm, bm, wlr, blr


def _hybrid_layer(h32, h16, mem16, B, L, depth, w_in, b_in, na_rpb, hy_conv_w, hy_conv_b, hy_w1, hy_b1,
                  hy_wm, hy_bm, hy_w3, hy_skip, gla_up_f, gla_bias_f, gla_up_b, gla_bias_b, gla_norm_g,
                  mem_w_kv, w_br, w_out, b_out, ln_g, ln_b):
    D = h32.shape[-1]
    rows = L // GRID_W
    assert rows % NA_G == 0 and rows >= NA_KR and L % GLA_T == 0
    wm, bm, wlr, blr = _permute_in_proj(w_in, b_in)
    assert wm.shape[1] == N_MAIN
    P, LR = in_proj(h16, wm, bm, wlr, blr)

    ng = rows // NA_G
    geo = [_na_group_geometry(g, rows) for g in range(ng)]
    assert all(geo[g][2] == geo[1][2] for g in range(1, ng - 1))
    rpb_flat = na_rpb.reshape(-1)
    bias = jnp.stack([na_bias_tile(rpb_flat, geo[g][2]) for g in (0, 1, ng - 1)], axis=0)
    o_na = na_attention(P, bias, B, L)

    o_hy = hyena_branch(P, B, L, hy_conv_w, hy_conv_b, hy_w1, hy_b1, hy_wm, hy_bm, hy_w3, hy_skip)

    HK = GLA_HEADS * GLA_DK
    up_f = jnp.zeros((LR_PAD, HK), F32).at[:GLA_RANK].set(gla_up_f)
    up_b = jnp.zeros((LR_PAD, HK), F32).at[GLA_RANK:2 * GLA_RANK].set(gla_up_b)
    o_b = gla_direction(P, LR, up_b, gla_bias_b.reshape(1, HK), B, L, reverse=True)
    o_gla = gla_direction(P, LR, up_f, gla_bias_f.reshape(1, HK), B, L, reverse=False,
                          o_other=o_b, norm_g=gla_norm_g.reshape(1, -1))

    n_mem = mem16.shape[0] // B
    kv = matmul_bf16(mem16, mem_w_kv.astype(BF16))
    o_mem = mem_attention(P, kv, B, L, n_mem)

    alpha = (2 * depth) ** 0.25
    return merge_layer(o_na, o_hy, o_gla, o_mem, P, (0.5 * w_br).astype(BF16), w_out.astype(BF16),
                       b_out.reshape(1, D), h32, ln_g.reshape(1, D), ln_b.reshape(1, D), alpha)


def kernel(x, mem, ln_in_g, ln_in_b, w_in, b_in, na_rpb, hy_conv_w, hy_conv_b, hy_w1, hy_b1, hy_wm, hy_bm,
           hy_w3, hy_skip, gla_up_f, gla_bias_f, gla_up_b, gla_bias_b, gla_norm_g, mem_w_kv, w_br, w_out,
           b_out, ln_g, ln_b):
    B, L, D = x.shape
    depth = w_in.shape[0]
    h32, h16 = ln_in(x.reshape(B * L, D), ln_in_g, ln_in_b)
    mem16 = mem.reshape(-1, D).astype(BF16)
    for i in range(depth):
        h32, h16 = _hybrid_layer(
            h32, h16, mem16, B, L, depth, w_in[i], b_in[i], na_rpb[i], hy_conv_w[i], hy_conv_b[i], hy_w1[i],
            hy_b1[i], hy_wm[i], hy_bm[i], hy_w3[i], hy_skip[i], gla_up_f[i], gla_bias_f[i], gla_up_b[i],
            gla_bias_b[i], gla_norm_g[i], mem_w_kv[i], w_br[i], w_out[i], b_out[i], ln_g[i], ln_b[i])
    return h32.reshape(B, L, D)
```

```python
import functools
import math

import numpy as np
import jax
import jax.numpy as jnp
from jax import lax
from jax.experimental import pallas as pl
from jax.experimental.pallas import tpu as pltpu

F32 = jnp.float32
BF16 = jnp.bfloat16

GRID_W = 64
NA_HEADS, NA_HEAD_DIM, NA_WIN_R, NA_WIN_C = 8, 64, 8, 16
HY_W, HY_CONV, HY_EMB_BANDS, HY_FILTER_HID, HY_INNER_MLPS = 512, 3, 16, 64, 2
HY_EMB = 1 + 2 * HY_EMB_BANDS
HY_SIN_FREQ = 1.0
HY_DECAY_TARGET, HY_FAST_DECAY_PCT, HY_SLOW_DECAY_PCT = 1e-2, 0.3, 1.5
HY_MAX_DECAY = math.log(HY_DECAY_TARGET) / HY_FAST_DECAY_PCT
HY_MIN_DECAY = math.log(HY_DECAY_TARGET) / HY_SLOW_DECAY_PCT
GLA_HEADS, GLA_DK, GLA_DV, GLA_RANK, GLA_TAU, GLA_CHUNK = 4, 64, 128, 16, 16.0, 64
MEM_HEADS, MEM_HEAD_DIM = 4, 128
N_BRANCH, BRANCH_W = 4, 512
LN_EPS, RMS_EPS = 1e-5, 1e-6

LANES = 128
VMEM_LIMIT = 52 * 1024 * 1024
NEG_BIG = -1e30

COL_G = 0
COL_Z = 4096
COL_NAQ, COL_NAK, COL_NAV = 6144, 6656, 7168
COL_HYV, COL_HYX0, COL_HYX1 = 7680, 8192, 8704
COL_GQ, COL_GK, COL_GV = 9216, 9472, 9728
COL_MQ = 10240
N_MAIN = 10752
LR_PAD = 128

LOG2E = math.log2(math.e)
NA_QSCALE = NA_HEAD_DIM ** -0.5 * LOG2E
NA_G = 4
NA_KR = NA_G + NA_WIN_R - 1
NA_GROUPS_PER_STEP = 8
GLA_T = 256
GLA_BATCH_PER_STEP = 4
DFT_N2 = 128
DFT_T2_BLOCK = 16


def _cp(sem):
    return pltpu.CompilerParams(dimension_semantics=sem, vmem_limit_bytes=VMEM_LIMIT)


def _split_bf16(a):
    hi = a.astype(BF16)
    lo = (a - hi.astype(F32)).astype(BF16)
    return hi, lo


def _dot(a, b):
    return jnp.dot(a, b, preferred_element_type=F32)


def _dot3(a, b):
    ah, al = _split_bf16(a)
    bh, bl = _split_bf16(b)
    return _dot(ah, bh) + _dot(al, bh) + _dot(ah, bl)


def _dot_nt(a, b):
    return lax.dot_general(a, b, (((1,), (1,)), ((), ())), preferred_element_type=F32)


def _dot_tn(a, b):
    return lax.dot_general(a, b, (((0,), (0,)), ((), ())), preferred_element_type=F32)


def _layer_norm_rows(y, g, b):
    mu = jnp.mean(y, axis=-1, keepdims=True)
    d = y - mu
    var = jnp.mean(d * d, axis=-1, keepdims=True)
    return d * lax.rsqrt(var + LN_EPS) * g + b


def _ln_in_body(x_ref, g_ref, b_ref, h32_ref, h16_ref):
    h = _layer_norm_rows(x_ref[...], g_ref[...], b_ref[...])
    h32_ref[...] = h
    h16_ref[...] = h.astype(BF16)


def ln_in(x2, g, b, tm=512):
    M, D = x2.shape
    row = pl.BlockSpec((tm, D), lambda i: (i, 0))
    vec = pl.BlockSpec((1, D), lambda i: (0, 0))
    return pl.pallas_call(
        _ln_in_body,
        grid=(M // tm,),
        in_specs=[row, vec, vec],
        out_specs=[row, row],
        out_shape=[jax.ShapeDtypeStruct((M, D), F32), jax.ShapeDtypeStruct((M, D), BF16)],
        compiler_params=_cp(("parallel",)),
        name="ln_in",
    )(x2, g.reshape(1, D), b.reshape(1, D))


def _in_proj_body(x_ref, w_ref, b_ref, wlr_ref, blr_ref, p_ref, lr_ref):
    x = x_ref[...]
    p_ref[...] = (_dot(x, w_ref[...]) + b_ref[...]).astype(BF16)

    @pl.when(pl.program_id(1) == 0)
    def _():
        lr_ref[...] = _dot(x, wlr_ref[...]) + blr_ref[...]


def in_proj(h16, wm, bm, wlr, blr, tm=2048, tn=1536):
    M, D = h16.shape
    NM = wm.shape[1]
    return pl.pallas_call(
        _in_proj_body,
        grid=(M // tm, NM // tn),
        in_specs=[
            pl.BlockSpec((tm, D), lambda i, j: (i, 0)),
            pl.BlockSpec((D, tn), lambda i, j: (0, j)),
            pl.BlockSpec((1, tn), lambda i, j: (0, j)),
            pl.BlockSpec((D, LR_PAD), lambda i, j: (0, 0)),
            pl.BlockSpec((1, LR_PAD), lambda i, j: (0, 0)),
        ],
        out_specs=[
            pl.BlockSpec((tm, tn), lambda i, j: (i, j)),
            pl.BlockSpec((tm, LR_PAD), lambda i, j: (i, 0)),
        ],
        out_shape=[jax.ShapeDtypeStruct((M, NM), BF16), jax.ShapeDtypeStruct((M, LR_PAD), F32)],
        compiler_params=_cp(("parallel", "arbitrary")),
        name="in_proj",
    )(h16, wm, bm, wlr, blr)


def _matmul_body(x_ref, w_ref, o_ref):
    o_ref[...] = _dot(x_ref[...], w_ref[...]).astype(o_ref.dtype)


def matmul_bf16(x, w, tm=512):
    M, K = x.shape
    N = w.shape[1]
    return pl.pallas_call(
        _matmul_body,
        grid=(M // tm,),
        in_specs=[pl.BlockSpec((tm, K), lambda i: (i, 0)), pl.BlockSpec((K, N), lambda i: (0, 0))],
        out_specs=pl.BlockSpec((tm, N), lambda i: (i, 0)),
        out_shape=jax.ShapeDtypeStruct((M, N), BF16),
        compiler_params=_cp(("parallel",)),
        name="mem_kv",
    )(x, w)


def _na_row_start(r, rows):
    return min(max(r - NA_WIN_R // 2, 0), rows - NA_WIN_R)


def _na_group_geometry(g, rows):
    r0 = g * NA_G
    kb = min(max(r0 - NA_WIN_R // 2, 0), rows - NA_KR)
    pat = []
    for i in range(NA_G):
        rs = _na_row_start(r0 + i, rows)
        pat.append(tuple((kb + j - (r0 + i) + NA_WIN_R - 1) if rs <= kb + j < rs + NA_WIN_R else None
                         for j in range(NA_KR)))
    return r0, kb, tuple(pat)


def _na_bias_body(rpb_ref, o_ref, *, pattern):
    h = pl.program_id(0)
    n_dc = 2 * NA_WIN_C - 1
    base = h * ((2 * NA_WIN_R - 1) * n_dc)
    cc = lax.broadcasted_iota(jnp.int32, (GRID_W, LANES), 0)
    lane = lax.broadcasted_iota(jnp.int32, (GRID_W, LANES), 1)
    c = lane % GRID_W
    left = lane < GRID_W
    cs = jnp.clip(c - NA_WIN_C // 2, 0, GRID_W - NA_WIN_C)
    col_ok = (cc >= cs) & (cc < cs + NA_WIN_C)
    dcol = cc - c + NA_WIN_C - 1
    neg = jnp.full((GRID_W, LANES), NEG_BIG, F32)
    cache = {}

    def pair_tile(a0, a1):
        key = (a0, a1)
        if key in cache:
            return cache[key]
        t = neg
        for d in range(n_dc):
            v0 = rpb_ref[base + a0 * n_dc + d] * LOG2E if a0 is not None else NEG_BIG
            v1 = rpb_ref[base + a1 * n_dc + d] * LOG2E if a1 is not None else NEG_BIG
            val = jnp.where(left, v0, v1)
            t = jnp.where(col_ok & (dcol == d), val, t)
        cache[key] = t
        return t

    assert NA_G % 2 == 0
    for j in range(NA_KR):
        for ip in range(NA_G // 2):
            a0, a1 = pattern[2 * ip][j], pattern[2 * ip + 1][j]
            t = neg if (a0 is None and a1 is None) else pair_tile(a0, a1)
            o_ref[0, j * GRID_W:(j + 1) * GRID_W, ip * LANES:(ip + 1) * LANES] = t


def na_bias_tile(rpb_flat, pattern):
    tq, tk = NA_G * GRID_W, NA_KR * GRID_W
    return pl.pallas_call(
        functools.partial(_na_bias_body, pattern=pattern),
        grid=(NA_HEADS,),
        in_specs=[pl.BlockSpec(memory_space=pltpu.SMEM)],
        out_specs=pl.BlockSpec((1, tk, tq), lambda h: (h, 0, 0)),
        out_shape=jax.ShapeDtypeStruct((NA_HEADS, tk, tq), F32),
        compiler_params=_cp(("parallel",)),
        name="na_bias",
    )(rpb_flat)


def _na_attn_body(q_ref, k_ref, v_ref, *rest, rows):
    (bias_first, bias_mid, bias_last), o_ref = rest[:-1], rest[-1]
    tq, tk = NA_G * GRID_W, NA_KR * GRID_W
    n_grp = q_ref.shape[0] // tq
    bias_refs = [bias_first] + [bias_mid] * (n_grp - 2) + [bias_last]
    g0 = pl.program_id(2) * n_grp
    kws, vws, qms, biases = [], [], [], []
    for a in range(n_grp):
        kb = jnp.clip((g0 + a) * NA_G - NA_WIN_R // 2, 0, rows - NA_KR)
        start = pl.multiple_of(kb * GRID_W, GRID_W)
        kw = k_ref[pl.ds(start, tk), :]
        vw = v_ref[pl.ds(start, tk), :]
        q = q_ref[a * tq:(a + 1) * tq, :]
        first = lax.broadcasted_iota(jnp.int32, q.shape, 1) < NA_HEAD_DIM
        zero = jnp.zeros_like(q)
        for hh in range(2):
            kws.append(kw)
            vws.append(vw)
            qms.append(jnp.where(first if hh == 0 else jnp.logical_not(first), q, zero))
            biases.append(bias_refs[a][0, hh])
    sts = [_dot_nt(kw, qm) + b for kw, qm, b in zip(kws, qms, biases)]
    ms = [jnp.max(st, axis=0, keepdims=True) for st in sts]
    ps = [jnp.exp2(st - m) for st, m in zip(sts, ms)]
    ls = [jnp.sum(p, axis=0, keepdims=True) for p in ps]
    outs = [_dot_tn(vw, p.astype(BF16)) / l for vw, p, l in zip(vws, ps, ls)]
    top = lax.broadcasted_iota(jnp.int32, outs[0].shape, 0) < NA_HEAD_DIM
    for a in range(n_grp):
        o_ref[a * tq:(a + 1) * tq, :] = jnp.where(top, outs[2 * a], outs[2 * a + 1]).T.astype(BF16)


def na_attention(P, bias, B, L):
    rows = L // GRID_W
    ng = rows // NA_G
    n_grp = max(d for d in range(2, NA_GROUPS_PER_STEP + 1) if ng % d == 0)
    ns = ng // n_grp
    tq, tk = NA_G * GRID_W, NA_KR * GRID_W
    hp_n = NA_HEADS // 2
    qb, kb_, vb = COL_NAQ // LANES, COL_NAK // LANES, COL_NAV // LANES
    bias_specs = [
        pl.BlockSpec((1, 2, tk, tq), lambda b, hp, s: (jnp.where(s == 0, 0, 1), hp, 0, 0)),
        pl.BlockSpec((1, 2, tk, tq), lambda b, hp, s: (1, hp, 0, 0)),
        pl.BlockSpec((1, 2, tk, tq), lambda b, hp, s: (jnp.where(s == ns - 1, 2, 1), hp, 0, 0)),
    ]

    return pl.pallas_call(
        functools.partial(_na_attn_body, rows=rows),
        grid=(B, hp_n, ns),
        in_specs=[
            pl.BlockSpec((n_grp * tq, LANES), lambda b, hp, s: (b * ns + s, qb + hp)),
            pl.BlockSpec((L, LANES), lambda b, hp, s: (b, kb_ + hp)),
            pl.BlockSpec((L, LANES), lambda b, hp, s: (b, vb + hp)),
        ] + bias_specs,
        out_specs=pl.BlockSpec((n_grp * tq, LANES), lambda b, hp, s: (b * ns + s, hp)),
        out_shape=jax.ShapeDtypeStruct((B * L, NA_HEADS * NA_HEAD_DIM), BF16),
        compiler_params=_cp(("parallel", "parallel", "arbitrary")),
        name="na_attn",
    )(P, P, P, bias, bias, bias)


def _mem_attn_body(q_ref, kv_ref, o_ref):
    hd = MEM_HEAD_DIM
    heads = range(MEM_HEADS)
    ss = [_dot_nt(q_ref[:, h * hd:(h + 1) * hd], kv_ref[:, h * hd:(h + 1) * hd]) * (hd ** -0.5) for h in heads]
    ms = [jnp.max(s, axis=-1, keepdims=True) for s in ss]
    ps = [jnp.exp(s - m) for s, m in zip(ss, ms)]
    ls = [jnp.sum(p, axis=-1, keepdims=True) for p in ps]
    outs = [_dot(ps[h].astype(BF16), kv_ref[:, (MEM_HEADS + h) * hd:(MEM_HEADS + h + 1) * hd]) / ls[h]
            for h in heads]
    o_ref[...] = jnp.concatenate(outs, axis=-1).astype(BF16)


def mem_attention(P, kv, B, L, n_mem, tq=1024):
    W = MEM_HEADS * MEM_HEAD_DIM
    nt = L // tq
    return pl.pallas_call(
        _mem_attn_body,
        grid=(B, nt),
        in_specs=[
            pl.BlockSpec((tq, W), lambda b, t: (b * nt + t, COL_MQ // W)),
            pl.BlockSpec((n_mem, 2 * W), lambda b, t: (b, 0)),
        ],
        out_specs=pl.BlockSpec((tq, W), lambda b, t: (b * nt + t, 0)),
        out_shape=jax.ShapeDtypeStruct((B * L, W), BF16),
        compiler_params=_cp(("parallel", "parallel")),
        name="mem_attn",
    )(P, kv)


def _log_sigmoid(x):
    return jnp.minimum(x, 0.0) - jnp.log(1.0 + jnp.exp(-jnp.abs(x)))


def _gla_body(*refs, reverse, combine):
    if combine:
        q_ref, k_ref, v_ref, lr_ref, up_ref, gb_ref, ob_ref, ng_ref, o_ref, st_ref = refs
    else:
        q_ref, k_ref, v_ref, lr_ref, up_ref, gb_ref, o_ref, st_ref = refs
    T, C = GLA_T, GLA_CHUNK
    n_sub = T // C
    HK = GLA_HEADS * GLA_DK
    HV = GLA_HEADS * GLA_DV

    @pl.when(pl.program_id(1) == 0)
    def _():
        st_ref[...] = jnp.zeros_like(st_ref)

    r = lax.broadcasted_iota(jnp.int32, (T, T), 0)
    s = lax.broadcasted_iota(jnp.int32, (T, T), 1)
    same = (r // C) == (s // C)
    if reverse:
        csum_mask = same & (s >= r)
        att_mask = same & (s > r)
    else:
        csum_mask = same & (s <= r)
        att_mask = same & (s <= r)
    tri = jnp.where(csum_mask, 1.0, 0.0).astype(BF16)
    lane = lax.broadcasted_iota(jnp.int32, (T, HK), 1)
    er = lax.broadcasted_iota(jnp.int32, (HV, HK), 0) // GLA_DV
    dc = lax.broadcasted_iota(jnp.int32, (HV, HK), 1) // GLA_DK
    bd = er == dc
    edge_row = 0 if reverse else C - 1
    mid_row = C // 2 if reverse else C // 2 - 1

    nbb = q_ref.shape[0]
    bbs = range(nbb)
    heads = range(GLA_HEADS)
    hms = [(lane // GLA_DK) == h for h in heads]
    las = [_log_sigmoid(_dot3(lr_ref[bb], up_ref[...]) + gb_ref[...]) * (1.0 / GLA_TAU) for bb in bbs]
    splits = [_split_bf16(la) for la in las]
    bcs = [_dot(tri, hi) + _dot(tri, lo) for hi, lo in splits]
    edges = [jnp.concatenate([jnp.broadcast_to(bc[i * C + edge_row:i * C + edge_row + 1], (C, HK))
                              for i in range(n_sub)], axis=0) for bc in bcs]
    mids = [jnp.concatenate([jnp.broadcast_to(bc[i * C + mid_row:i * C + mid_row + 1], (C, HK))
                             for i in range(n_sub)], axis=0) for bc in bcs]
    qs = [q_ref[bb].astype(F32) * (GLA_DK ** -0.5) for bb in bbs]
    ks = [k_ref[bb].astype(F32) for bb in bbs]
    vs = [v_ref[bb] for bb in bbs]
    qes = [(q * jnp.exp(bc - mid)).astype(BF16) for q, bc, mid in zip(qs, bcs, mids)]
    kes = [(k * jnp.exp(mid - bc)).astype(BF16) for k, bc, mid in zip(ks, bcs, mids)]
    atts = [[_dot_nt(jnp.where(hms[h], qes[bb], jnp.zeros_like(qes[bb])), kes[bb]) for h in heads] for bb in bbs]
    qbs = [(q * jnp.exp(bc)).astype(BF16) for q, bc in zip(qs, bcs)]
    kls = [(k * jnp.exp(edge - bc)).astype(BF16) for k, bc, edge in zip(ks, bcs, edges)]
    decs = [[jnp.exp(edge[i * C:i * C + 1]) for i in range(n_sub)] for edge in edges]
    atts = [[jnp.where(att_mask, atts[bb][h], 0.0).astype(BF16) for h in heads] for bb in bbs]
    us = [[_dot_tn(vs[bb][i * C:(i + 1) * C], kls[bb][i * C:(i + 1) * C]) for i in range(n_sub)] for bb in bbs]
    intras = [[_dot(atts[bb][h], vs[bb][:, h * GLA_DV:(h + 1) * GLA_DV]) for h in heads] for bb in bbs]

    sts = [st_ref[bb] for bb in bbs]
    inters = [[None] * n_sub for _ in bbs]
    order = range(n_sub - 1, -1, -1) if reverse else range(n_sub)
    for i in order:
        for bb in bbs:
            inters[bb][i] = _dot_nt(qbs[bb][i * C:(i + 1) * C], sts[bb].astype(BF16))
            sts[bb] = decs[bb][i] * sts[bb] + jnp.where(bd, us[bb][i], 0.0)
    for bb in bbs:
        st_ref[bb] = sts[bb]
        o = jnp.concatenate(intras[bb], axis=-1) + jnp.concatenate(inters[bb], axis=0)
        if combine:
            o = o + ob_ref[bb]
            outs = []
            for h in heads:
                oh = o[:, h * GLA_DV:(h + 1) * GLA_DV]
                outs.append(oh * lax.rsqrt(jnp.mean(oh * oh, axis=-1, keepdims=True) + RMS_EPS))
            o_ref[bb] = (jnp.concatenate(outs, axis=-1) * ng_ref[...]).astype(o_ref.dtype)
        else:
            o_ref[bb] = o


def gla_direction(P, LR, up_pad, gbias, B, L, *, reverse, o_other=None, norm_g=None):
    T = GLA_T
    nb = L // T
    nbb = GLA_BATCH_PER_STEP if B % GLA_BATCH_PER_STEP == 0 else 1
    HK, HV = GLA_HEADS * GLA_DK, GLA_HEADS * GLA_DV
    combine = o_other is not None
    P3 = P.reshape(B, L, P.shape[-1])
    LR3 = LR.reshape(B, L, LR_PAD)

    def tok(j):
        return nb - 1 - j if reverse else j

    seq = lambda w, col: pl.BlockSpec((nbb, T, w), lambda b, j: (b, tok(j), col))
    in_specs = [
        seq(HK, COL_GQ // HK), seq(HK, COL_GK // HK), seq(HV, COL_GV // HV), seq(LR_PAD, 0),
        pl.BlockSpec((LR_PAD, HK), lambda b, j: (0, 0)),
        pl.BlockSpec((1, HK), lambda b, j: (0, 0)),
    ]
    args = [P3, P3, P3, LR3, up_pad, gbias]
    if combine:
        in_specs += [seq(HV, 0), pl.BlockSpec((1, HV), lambda b, j: (0, 0))]
        args += [o_other.reshape(B, L, HV), norm_g]
    out = pl.pallas_call(
        functools.partial(_gla_body, reverse=reverse, combine=combine),
        grid=(B // nbb, nb),
        in_specs=in_specs,
        out_specs=seq(HV, 0),
        out_shape=jax.ShapeDtypeStruct((B, L, HV), BF16 if combine else F32),
        scratch_shapes=[pltpu.VMEM((nbb, HV, HK), F32)],
        compiler_params=_cp(("parallel", "arbitrary")),
        name="gla_fwd" if combine else "gla_bwd",
    )(*args)
    return out.reshape(B * L, HV)


def _hy_conv_body(v_ref, x0_ref, x1_ref, wv_ref, wx0_ref, wx1_ref, bv_ref, bx0_ref, bx1_ref,
                  u_ref, x0o_ref, *, L, R):
    n_chunks = L // R
    HALO = 16
    rid = lax.broadcasted_iota(jnp.int32, (R, LANES), 0)

    def conv(ref, w_ref, b_ref, i):
        base = pl.multiple_of(i * R, R)
        cur = ref[pl.ds(base, R), :].astype(F32)
        pstart = pl.multiple_of(jnp.maximum(base - HALO, 0), HALO)
        nstart = pl.multiple_of(jnp.minimum(base + R, L - HALO), HALO)
        prev_row = ref[pl.ds(pstart, HALO), :][HALO - 1:HALO].astype(F32) * jnp.where(i > 0, 1.0, 0.0)
        next_row = ref[pl.ds(nstart, HALO), :][0:1].astype(F32) * jnp.where(i < n_chunks - 1, 1.0, 0.0)
        up = jnp.where(rid == 0, prev_row, pltpu.roll(cur, 1, 0))
        dn = jnp.where(rid == R - 1, next_row, pltpu.roll(cur, R - 1, 0))
        w = w_ref[...]
        return b_ref[...] + up * w[0:1] + cur * w[1:2] + dn * w[2:3]

    def step(i, carry):
        base = pl.multiple_of(i * R, R)
        cv = conv(v_ref, wv_ref, bv_ref, i)
        cx1 = conv(x1_ref, wx1_ref, bx1_ref, i)
        u_ref[pl.ds(base, R), :] = (cx1 * cv).astype(BF16)
        x0o_ref[pl.ds(base, R), :] = conv(x0_ref, wx0_ref, bx0_ref, i).astype(BF16)
        return carry

    lax.fori_loop(0, n_chunks, step, 0)


def hy_short_conv(P, conv_w, conv_b, B, L, R=1024):
    nc = HY_W // LANES
    bv, bx0, bx1 = COL_HYV // LANES, COL_HYX0 // LANES, COL_HYX1 // LANES
    seq = lambda off: pl.BlockSpec((L, LANES), lambda b, c: (b, off + c))
    wsp = lambda seg: pl.BlockSpec((HY_CONV, LANES), lambda b, c: (0, seg * nc + c))
    bsp = lambda seg: pl.BlockSpec((1, LANES), lambda b, c: (0, seg * nc + c))
    out = pl.BlockSpec((L, LANES), lambda b, c: (b, c))
    return pl.pallas_call(
        functools.partial(_hy_conv_body, L=L, R=min(R, L)),
        grid=(B, nc),
        in_specs=[seq(bv), seq(bx0), seq(bx1), wsp(0), wsp(1), wsp(2), bsp(0), bsp(1), bsp(2)],
        out_specs=[out, out],
        out_shape=[jax.ShapeDtypeStruct((B * L, HY_W), BF16)] * 2,
        compiler_params=_cp(("parallel", "parallel")),
        name="hy_conv",
    )(P, P, P, conv_w, conv_w, conv_w, conv_b, conv_b, conv_b)


def _hy_filter_body(freq_ref, delta_ref, w1_ref, b1_ref, wm_ref, bm_ref, w3_ref, k_ref, e_ref, *, L, R):
    i = pl.program_id(0)
    t = i * R + lax.broadcasted_iota(jnp.int32, (R, 1), 0)
    back = t >= L
    pos = jnp.where(back, 2 * L - t, t).astype(F32)
    t01 = pos / float(L - 1)
    lane = lax.broadcasted_iota(jnp.int32, (R, LANES), 1)
    ang = ((2.0 * math.pi / L) * pos) * freq_ref[...]
    z = jnp.where(lane == 0, t01,
                  jnp.where(lane <= HY_EMB_BANDS, jnp.cos(ang),
                            jnp.where(lane < HY_EMB, -jnp.sin(ang), 0.0)))
    h = jnp.sin(HY_SIN_FREQ * (_dot3(z, w1_ref[...]) + b1_ref[...]))
    for j in range(HY_INNER_MLPS):
        h = jnp.sin(HY_SIN_FREQ * (_dot3(h, wm_ref[j]) + bm_ref[j]))
    hw = _dot3(h, w3_ref[...])
    window = jnp.exp(-t01 * delta_ref[...])
    taps = jnp.where(back, hw[:, HY_W:], hw[:, :HY_W]) * window
    taps = jnp.where(t == L, 0.0, taps)
    k_ref[...] = taps.astype(BF16)

    @pl.when(i == 0)
    def _():
        e_ref[...] = jnp.zeros_like(e_ref)

    e_ref[...] += jnp.sum(taps * taps, axis=0, keepdims=True)


def hy_filter(freq, delta, w1p, b1p, wmp, bmp, w3p, L, R=1024):
    R = min(R, L)
    full = lambda a: pl.BlockSpec(a.shape, lambda i, _n=a.ndim: (0,) * _n)
    args = (freq, delta, w1p, b1p, wmp, bmp, w3p)
    return pl.pallas_call(
        functools.partial(_hy_filter_body, L=L, R=R),
        grid=(2 * L // R,),
        in_specs=[full(a) for a in args],
        out_specs=[pl.BlockSpec((R, HY_W), lambda i: (i, 0)), pl.BlockSpec((1, HY_W), lambda i: (0, 0))],
        out_shape=[jax.ShapeDtypeStruct((2 * L, HY_W), BF16), jax.ShapeDtypeStruct((1, HY_W), F32)],
        compiler_params=_cp(("arbitrary",)),
        name="hy_filter",
    )(*args)


def _dft_slow_body(c_ref, x_ref, o_ref):
    two, half, t2n, w = x_ref.shape
    x = jnp.swapaxes(x_ref[...].reshape(two * half, t2n, w), 0, 1)
    c = c_ref[...]
    y = jnp.stack([_dot(c, x[t]).astype(BF16) for t in range(t2n)], axis=0)
    o_ref[...] = jnp.swapaxes(y, 0, 1).reshape(o_ref.shape)


def dft_slow(cmat, x5, t2n=DFT_T2_BLOCK):
    R, K = cmat.shape
    npair, _, half, n2, w = x5.shape
    assert K == 2 * half and n2 % t2n == 0
    return pl.pallas_call(
        _dft_slow_body,
        grid=(npair, n2 // t2n),
        in_specs=[
            pl.BlockSpec((R, K), lambda p, j: (0, 0)),
            pl.BlockSpec((None, 2, half, t2n, w), lambda p, j: (p, 0, 0, j, 0)),
        ],
        out_specs=pl.BlockSpec((None, 2, R // 2, t2n, w), lambda p, j: (p, 0, 0, j, 0)),
        out_shape=jax.ShapeDtypeStruct((npair, 2, R // 2, n2, w), BF16),
        compiler_params=_cp(("parallel", "parallel")),
        name="dft_slow",
    )(cmat, x5)


def _stack_complex(tab_ref, k):
    cr, ci = tab_ref[k, 0], tab_ref[k, 1]
    return jnp.concatenate([jnp.concatenate([cr, -ci], axis=1), jnp.concatenate([ci, cr], axis=1)], axis=0)


def _hy_kf_body(a_ref, cf_ref, e_ref, o_ref, *, F, n_fft):
    n2 = DFT_N2
    scale = lax.rsqrt(e_ref[...] + 1e-6) * (1.0 / n_fft)
    for k in range(F):
        a = jnp.concatenate([a_ref[0, k], a_ref[1, k]], axis=0)
        x = _dot(_stack_complex(cf_ref, k), a) * scale
        o_ref[k, 0] = x[:n2]
        o_ref[k, 1] = x[n2:]


def hy_filter_spectrum(a5, cf_tab, energy, n_fft, F=8):
    n1, n2 = n_fft // DFT_N2, DFT_N2
    F = min(F, n1)
    return pl.pallas_call(
        functools.partial(_hy_kf_body, F=F, n_fft=n_fft),
        grid=(n1 // F,),
        in_specs=[
            pl.BlockSpec((2, F, n2, HY_W), lambda f: (0, f, 0, 0)),
            pl.BlockSpec((F, 2, n2, n2), lambda f: (f, 0, 0, 0)),
            pl.BlockSpec((1, HY_W), lambda f: (0, 0)),
        ],
        out_specs=pl.BlockSpec((F, 2, n2, HY_W), lambda f: (f, 0, 0, 0)),
        out_shape=jax.ShapeDtypeStruct((n1, 2, n2, HY_W), F32),
        compiler_params=_cp(("parallel",)),
        name="hy_kf",
    )(a5, cf_tab, energy)


def _hy_mid_body(a_ref, kf_ref, cf_ref, ci_ref, o_ref, *, F):
    n2 = DFT_N2
    ks = range(F)
    xs = [_dot(_stack_complex(cf_ref, k), jnp.concatenate([a_ref[0, k], a_ref[1, k]], axis=0)) for k in ks]
    ys = []
    for k in ks:
        xr, xi = xs[k][:n2], xs[k][n2:]
        kr, ki = kf_ref[k, 0], kf_ref[k, 1]
        ys.append(jnp.concatenate([xr * kr - xi * ki, xr * ki + xi * kr], axis=0).astype(BF16))
    bvs = [_dot(_stack_complex(ci_ref, k), ys[k]) for k in ks]
    for k in ks:
        o_ref[0, k] = bvs[k][:n2].astype(BF16)
        o_ref[1, k] = bvs[k][n2:].astype(BF16)


def hy_mid(a5, kf, cf_tab, ci_tab, F=8):
    npair = a5.shape[0]
    n1, n2 = kf.shape[0], DFT_N2
    F = min(F, n1)
    return pl.pallas_call(
        functools.partial(_hy_mid_body, F=F),
        grid=(n1 // F, npair),
        in_specs=[
            pl.BlockSpec((None, 2, F, n2, HY_W), lambda f, p: (p, 0, f, 0, 0)),
            pl.BlockSpec((F, 2, n2, HY_W), lambda f, p: (f, 0, 0, 0)),
            pl.BlockSpec((F, 2, n2, n2), lambda f, p: (f, 0, 0, 0)),
            pl.BlockSpec((F, 2, n2, n2), lambda f, p: (f, 0, 0, 0)),
        ],
        out_specs=pl.BlockSpec((None, 2, F, n2, HY_W), lambda f, p: (p, 0, f, 0, 0)),
        out_shape=jax.ShapeDtypeStruct((npair, 2, n1, n2, HY_W), BF16),
        compiler_params=_cp(("parallel", "arbitrary")),
        name="hy_mid",
    )(a5, kf, cf_tab, ci_tab)


def _hy_out_body(c_ref, b_ref, u_ref, x0_ref, skip_ref, o_ref):
    two, n1, t2n, w = b_ref.shape
    b = jnp.swapaxes(b_ref[...].reshape(two * n1, t2n, w), 0, 1)
    c = c_ref[...]
    y = jnp.stack([_dot(c, b[t]) for t in range(t2n)], axis=0)
    y = jnp.swapaxes(y, 0, 1).reshape(o_ref.shape)
    conv = y + skip_ref[...] * u_ref[...].astype(F32)
    o_ref[...] = (x0_ref[...].astype(F32) * conv).astype(BF16)


def hy_out(cmat, bsp, u5, x05, skip, t2n=DFT_T2_BLOCK):
    npair, _, n1, n2, w = bsp.shape
    half = u5.shape[2]
    pair = pl.BlockSpec((None, 2, half, t2n, w), lambda p, j: (p, 0, 0, j, 0))
    return pl.pallas_call(
        _hy_out_body,
        grid=(npair, n2 // t2n),
        in_specs=[
            pl.BlockSpec((2 * half, 2 * n1), lambda p, j: (0, 0)),
            pl.BlockSpec((None, 2, n1, t2n, w), lambda p, j: (p, 0, 0, j, 0)),
            pair, pair,
            pl.BlockSpec((1, w), lambda p, j: (0, 0)),
        ],
        out_specs=pair,
        out_shape=jax.ShapeDtypeStruct((npair, 2, half, n2, w), BF16),
        compiler_params=_cp(("parallel", "parallel")),
        name="hy_out",
    )(cmat, bsp, u5, x05, skip)


@functools.lru_cache(maxsize=None)
def _dft_tables(L):
    n = 2 * L
    n2 = DFT_N2
    n1 = n // n2
    hl = n1 // 2
    f1 = np.arange(n1)
    t1 = np.arange(n1)
    th1 = 2.0 * np.pi * ((f1[:, None] * t1[None, :]) % n1) / n1
    fr, fi = np.cos(th1), -np.sin(th1)
    c1_data = np.block([[fr[:, :hl], -fi[:, :hl]], [fi[:, :hl], fr[:, :hl]]])
    c1_real = np.concatenate([fr, fi], axis=0)
    gr, gi = np.cos(th1.T)[:hl], np.sin(th1.T)[:hl]
    c1_inv = np.block([[gr, -gi], [gi, gr]])
    f2 = np.arange(n2)
    t2 = np.arange(n2)
    k = (t2[None, None, :] * (f1[:, None, None] + n1 * f2[None, :, None])) % n
    th = 2.0 * np.pi * k / n
    cf = np.stack([np.cos(th), -np.sin(th)], axis=1)
    tht = np.transpose(th, (0, 2, 1))
    ci = np.stack([np.cos(tht), np.sin(tht)], axis=1)
    return tuple(a.astype(np.float32).astype(BF16) for a in (c1_data, c1_real, c1_inv, cf, ci))


def hyena_branch(P, B, L, conv_w, conv_b, w1, b1, wm, bm, w3, skip):
    assert B % 2 == 0 and (2 * L) % DFT_N2 == 0 and L % DFT_N2 == 0
    n_fft = 2 * L
    n1, n2 = n_fft // DFT_N2, DFT_N2
    hl = n1 // 2
    NL = n2 * HY_W
    c1_data, c1_real, c1_inv, cf_tab, ci_tab = (jnp.asarray(a) for a in _dft_tables(L))

    hid = LANES
    freqs = np.linspace(1e-4, HY_EMB_BANDS - 1, HY_EMB_BANDS, dtype=np.float32)
    freq_lane = np.zeros((1, LANES), np.float32)
    freq_lane[0, 1:1 + HY_EMB_BANDS] = freqs
    freq_lane[0, 1 + HY_EMB_BANDS:HY_EMB] = freqs
    delta = np.abs(np.linspace(HY_MIN_DECAY, HY_MAX_DECAY, HY_W, dtype=np.float32)).reshape(1, HY_W)
    w1p = jnp.zeros((LANES, hid), F32).at[:HY_EMB, :HY_FILTER_HID].set(w1)
    b1p = jnp.zeros((1, hid), F32).at[0, :HY_FILTER_HID].set(b1)
    wmp = jnp.zeros((HY_INNER_MLPS, hid, hid), F32).at[:, :HY_FILTER_HID, :HY_FILTER_HID].set(wm)
    bmp = jnp.zeros((HY_INNER_MLPS, 1, hid), F32).at[:, 0, :HY_FILTER_HID].set(bm)
    w3p = jnp.zeros((hid, 2 * HY_W), F32).at[:HY_FILTER_HID].set(w3)
    kern, energy = hy_filter(jnp.asarray(freq_lane), jnp.asarray(delta), w1p, b1p, wmp, bmp, w3p, L)
    ak = dft_slow(c1_real, kern.reshape(1, 2, hl, n2, HY_W))[0]
    kf = hy_filter_spectrum(ak, cf_tab, energy, n_fft)

    u, x0c = hy_short_conv(P, conv_w, conv_b.reshape(1, -1), B, L)
    u5 = u.reshape(B // 2, 2, hl, n2, HY_W)
    x05 = x0c.reshape(B // 2, 2, hl, n2, HY_W)
    a = dft_slow(c1_data, u5)
    bsp = hy_mid(a, kf, cf_tab, ci_tab)
    return hy_out(c1_inv, bsp, u5, x05, skip.reshape(1, HY_W)).reshape(B * L, HY_W)


def _merge_body(ona_ref, ohy_ref, ogl_ref, omem_ref, z_ref, g_ref, wbr_ref, wout_ref, bout_ref,
                x_ref, lg_ref, lb_ref, h32_ref, h16_ref, *, alpha):
    D = x_ref.shape[-1]
    branches = range(N_BRANCH)
    o_refs = (ona_ref, ohy_ref, ogl_ref, omem_ref)
    zs = [z_ref[:, j * BRANCH_W:(j + 1) * BRANCH_W].astype(F32) for j in branches]
    acts = [(o_refs[j][...].astype(F32) * (zs[j] * (1.0 + jnp.tanh(zs[j])))).astype(BF16) for j in branches]
    projs = [_dot(acts[j], wbr_ref[j]) for j in branches]
    gates = [1.0 + jnp.tanh(g_ref[:, j * D:(j + 1) * D].astype(F32)) for j in branches]
    merged = gates[0] * projs[0]
    for j in branches[1:]:
        merged = merged + gates[j] * projs[j]
    out = _dot(merged.astype(BF16), wout_ref[...]) + bout_ref[...]
    h = _layer_norm_rows(alpha * x_ref[...] + out, lg_ref[...], lb_ref[...])
    h32_ref[...] = h
    h16_ref[...] = h.astype(BF16)


def merge_layer(o_na, o_hy, o_gla, o_mem, P, wbr, wout, bout, x32, lg, lb, alpha, tm=512):
    M, D = x32.shape
    row = lambda w, blk=0: pl.BlockSpec((tm, w), lambda i, _b=blk: (i, _b))
    vec = pl.BlockSpec((1, D), lambda i: (0, 0))
    return pl.pallas_call(
        functools.partial(_merge_body, alpha=alpha),
        grid=(M // tm,),
        in_specs=[
            row(BRANCH_W), row(BRANCH_W), row(BRANCH_W), row(BRANCH_W),
            row(N_BRANCH * BRANCH_W, COL_Z // (N_BRANCH * BRANCH_W)),
            row(N_BRANCH * D, COL_G // (N_BRANCH * D)),
            pl.BlockSpec((N_BRANCH, BRANCH_W, D), lambda i: (0, 0, 0), pipeline_mode=pl.Buffered(1)),
            pl.BlockSpec((D, D), lambda i: (0, 0), pipeline_mode=pl.Buffered(1)),
            vec, row(D), vec, vec,
        ],
        out_specs=[row(D), row(D)],
        out_shape=[jax.ShapeDtypeStruct((M, D), F32), jax.ShapeDtypeStruct((M, D), BF16)],
        compiler_params=_cp(("parallel",)),
        name="merge",
    )(o_na, o_hy, o_gla, o_mem, P, P, wbr, wout, bout, x32, lg, lb)


def _permute_in_proj(w_in, b_in):
    sizes = (512, 512, 512, 1536, 256, 256, 512, GLA_RANK, GLA_RANK, 512, 2048, 4096)
    offs = np.concatenate([[0], np.cumsum(sizes)])
    seg = lambda a, i: a[..., offs[i]:offs[i + 1]]
    order = (11, 10, 0, 1, 2, 3, 4, 5, 6, 9)
    col_scale = {0: NA_QSCALE, 10: 0.5, 11: 0.5}
    scaled = lambda a, i: seg(a, i) * col_scale[i] if i in col_scale else seg(a, i)
    wm = jnp.concatenate([scaled(w_in, i) for i in order], axis=-1).astype(BF16)
    bm = jnp.concatenate([scaled(b_in, i) for i in order], axis=-1).reshape(1, -1)
    pad = LR_PAD - 2 * GLA_RANK
    wlr = jnp.pad(jnp.concatenate([seg(w_in, 7), seg(w_in, 8)], axis=-1), ((0, 0), (0, pad))).astype(BF16)
    blr = jnp.pad(jnp.concatenate([seg(b_in, 7), seg(b_in, 8)], axis=-1), ((0, pad),)).reshape(1, -1)
    return wm, bm, wlr, blr


def _hybrid_layer(h32, h16, mem16, B, L, depth, w_in, b_in, na_rpb, hy_conv_w, hy_conv_b, hy_w1, hy_b1,
                  hy_wm, hy_bm, hy_w3, hy_skip, gla_up_f, gla_bias_f, gla_up_b, gla_bias_b, gla_norm_g,
                  mem_w_kv, w_br, w_out, b_out, ln_g, ln_b):
    D = h32.shape[-1]
    rows = L // GRID_W
    assert rows % NA_G == 0 and rows >= NA_KR and L % GLA_T == 0
    wm, bm, wlr, blr = _permute_in_proj(w_in, b_in)
    assert wm.shape[1] == N_MAIN
    P, LR = in_proj(h16, wm, bm, wlr, blr)

    ng = rows // NA_G
    geo = [_na_group_geometry(g, rows) for g in range(ng)]
    assert all(geo[g][2] == geo[1][2] for g in range(1, ng - 1))
    rpb_flat = na_rpb.reshape(-1)
    bias = jnp.stack([na_bias_tile(rpb_flat, geo[g][2]) for g in (0, 1, ng - 1)], axis=0)
    o_na = na_attention(P, bias, B, L)

    o_hy = hyena_branch(P, B, L, hy_conv_w, hy_conv_b, hy_w1, hy_b1, hy_wm, hy_bm, hy_w3, hy_skip)

    HK = GLA_HEADS * GLA_DK
    up_f = jnp.zeros((LR_PAD, HK), F32).at[:GLA_RANK].set(gla_up_f)
    up_b = jnp.zeros((LR_PAD, HK), F32).at[GLA_RANK:2 * GLA_RANK].set(gla_up_b)
    o_b = gla_direction(P, LR, up_b, gla_bias_b.reshape(1, HK), B, L, reverse=True)
    o_gla = gla_direction(P, LR, up_f, gla_bias_f.reshape(1, HK), B, L, reverse=False,
                          o_other=o_b, norm_g=gla_norm_g.reshape(1, -1))

    n_mem = mem16.shape[0] // B
    kv = matmul_bf16(mem16, mem_w_kv.astype(BF16))
    o_mem = mem_attention(P, kv, B, L, n_mem)

    alpha = (2 * depth) ** 0.25
    return merge_layer(o_na, o_hy, o_gla, o_mem, P, (0.5 * w_br).astype(BF16), w_out.astype(BF16),
                       b_out.reshape(1, D), h32, ln_g.reshape(1, D), ln_b.reshape(1, D), alpha)


def kernel(x, mem, ln_in_g, ln_in_b, w_in, b_in, na_rpb, hy_conv_w, hy_conv_b, hy_w1, hy_b1, hy_wm, hy_bm,
           hy_w3, hy_skip, gla_up_f, gla_bias_f, gla_up_b, gla_bias_b, gla_norm_g, mem_w_kv, w_br, w_out,
           b_out, ln_g, ln_b):
    B, L, D = x.shape
    depth = w_in.shape[0]
    h32, h16 = ln_in(x.reshape(B * L, D), ln_in_g, ln_in_b)
    mem16 = mem.reshape(-1, D).astype(BF16)
    for i in range(depth):
        h32, h16 = _hybrid_layer(
            h32, h16, mem16, B, L, depth, w_in[i], b_in[i], na_rpb[i], hy_conv_w[i], hy_conv_b[i], hy_w1[i],
            hy_b1[i], hy_wm[i], hy_bm[i], hy_w3[i], hy_skip[i], gla_up_f[i], gla_bias_f[i], gla_up_b[i],
            gla_bias_b[i], gla_norm_g[i], mem_w_kv[i], w_br[i], w_out[i], b_out[i], ln_g[i], ln_b[i])
    return h32.reshape(B, L, D)
```

```python
import functools
import math

import numpy as np
import jax
import jax.numpy as jnp
from jax import lax
from jax.experimental import pallas as pl
from jax.experimental.pallas import tpu as pltpu

F32 = jnp.float32
BF16 = jnp.bfloat16

GRID_W = 64
NA_HEADS, NA_HEAD_DIM, NA_WIN_R, NA_WIN_C = 8, 64, 8, 16
HY_W, HY_CONV, HY_EMB_BANDS, HY_FILTER_HID, HY_INNER_MLPS = 512, 3, 16, 64, 2
HY_EMB = 1 + 2 * HY_EMB_BANDS
HY_SIN_FREQ = 1.0
HY_DECAY_TARGET, HY_FAST_DECAY_PCT, HY_SLOW_DECAY_PCT = 1e-2, 0.3, 1.5
HY_MAX_DECAY = math.log(HY_DECAY_TARGET) / HY_FAST_DECAY_PCT
HY_MIN_DECAY = math.log(HY_DECAY_TARGET) / HY_SLOW_DECAY_PCT
GLA_HEADS, GLA_DK, GLA_DV, GLA_RANK, GLA_TAU, GLA_CHUNK = 4, 64, 128, 16, 16.0, 64
MEM_HEADS, MEM_HEAD_DIM = 4, 128
N_BRANCH, BRANCH_W = 4, 512
LN_EPS, RMS_EPS = 1e-5, 1e-6

LANES = 128
VMEM_LIMIT = 52 * 1024 * 1024
NEG_BIG = -1e30

COL_G = 0
COL_Z = 4096
COL_NAQ, COL_NAK, COL_NAV = 6144, 6656, 7168
COL_HYV, COL_HYX0, COL_HYX1 = 7680, 8192, 8704
COL_GQ, COL_GK, COL_GV = 9216, 9472, 9728
COL_MQ = 10240
N_MAIN = 10752
LR_PAD = 128

LOG2E = math.log2(math.e)
NA_QSCALE = NA_HEAD_DIM ** -0.5 * LOG2E
NA_G = 4
NA_KR = NA_G + NA_WIN_R - 1
NA_GROUPS_PER_STEP = 8
GLA_T = 256
GLA_BATCH_PER_STEP = 8
DFT_N2 = 128
DFT_T2_BLOCK = 16


def _cp(sem):
    return pltpu.CompilerParams(dimension_semantics=sem, vmem_limit_bytes=VMEM_LIMIT)


def _split_bf16(a):
    hi = a.astype(BF16)
    lo = (a - hi.astype(F32)).astype(BF16)
    return hi, lo


def _dot(a, b):
    return jnp.dot(a, b, preferred_element_type=F32)


def _dot3(a, b):
    ah, al = _split_bf16(a)
    bh, bl = _split_bf16(b)
    return _dot(ah, bh) + _dot(al, bh) + _dot(ah, bl)


def _dot_nt(a, b):
    return lax.dot_general(a, b, (((1,), (1,)), ((), ())), preferred_element_type=F32)


def _dot_tn(a, b):
    return lax.dot_general(a, b, (((0,), (0,)), ((), ())), preferred_element_type=F32)


def _layer_norm_rows(y, g, b):
    mu = jnp.mean(y, axis=-1, keepdims=True)
    d = y - mu
    var = jnp.mean(d * d, axis=-1, keepdims=True)
    return d * lax.rsqrt(var + LN_EPS) * g + b


def _ln_in_body(x_ref, g_ref, b_ref, h32_ref, h16_ref):
    h = _layer_norm_rows(x_ref[...], g_ref[...], b_ref[...])
    h32_ref[...] = h
    h16_ref[...] = h.astype(BF16)


def ln_in(x2, g, b, tm=512):
    M, D = x2.shape
    row = pl.BlockSpec((tm, D), lambda i: (i, 0))
    vec = pl.BlockSpec((1, D), lambda i: (0, 0))
    return pl.pallas_call(
        _ln_in_body,
        grid=(M // tm,),
        in_specs=[row, vec, vec],
        out_specs=[row, row],
        out_shape=[jax.ShapeDtypeStruct((M, D), F32), jax.ShapeDtypeStruct((M, D), BF16)],
        compiler_params=_cp(("parallel",)),
        name="ln_in",
    )(x2, g.reshape(1, D), b.reshape(1, D))


def _in_proj_body(x_ref, w_ref, b_ref, wlr_ref, blr_ref, p_ref, lr_ref):
    x = x_ref[...]
    p_ref[...] = (_dot(x, w_ref[...]) + b_ref[...]).astype(BF16)

    @pl.when(pl.program_id(1) == 0)
    def _():
        lr_ref[...] = _dot(x, wlr_ref[...]) + blr_ref[...]


def in_proj(h16, wm, bm, wlr, blr, tm=2048, tn=1536):
    M, D = h16.shape
    NM = wm.shape[1]
    return pl.pallas_call(
        _in_proj_body,
        grid=(M // tm, NM // tn),
        in_specs=[
            pl.BlockSpec((tm, D), lambda i, j: (i, 0)),
            pl.BlockSpec((D, tn), lambda i, j: (0, j)),
            pl.BlockSpec((1, tn), lambda i, j: (0, j)),
            pl.BlockSpec((D, LR_PAD), lambda i, j: (0, 0)),
            pl.BlockSpec((1, LR_PAD), lambda i, j: (0, 0)),
        ],
        out_specs=[
            pl.BlockSpec((tm, tn), lambda i, j: (i, j)),
            pl.BlockSpec((tm, LR_PAD), lambda i, j: (i, 0)),
        ],
        out_shape=[jax.ShapeDtypeStruct((M, NM), BF16), jax.ShapeDtypeStruct((M, LR_PAD), F32)],
        compiler_params=_cp(("parallel", "arbitrary")),
        name="in_proj",
    )(h16, wm, bm, wlr, blr)


def _matmul_body(x_ref, w_ref, o_ref):
    o_ref[...] = _dot(x_ref[...], w_ref[...]).astype(o_ref.dtype)


def matmul_bf16(x, w, tm=512):
    M, K = x.shape
    N = w.shape[1]
    return pl.pallas_call(
        _matmul_body,
        grid=(M // tm,),
        in_specs=[pl.BlockSpec((tm, K), lambda i: (i, 0)), pl.BlockSpec((K, N), lambda i: (0, 0))],
        out_specs=pl.BlockSpec((tm, N), lambda i: (i, 0)),
        out_shape=jax.ShapeDtypeStruct((M, N), BF16),
        compiler_params=_cp(("parallel",)),
        name="mem_kv",
    )(x, w)


def _na_row_start(r, rows):
    return min(max(r - NA_WIN_R // 2, 0), rows - NA_WIN_R)


def _na_group_geometry(g, rows):
    r0 = g * NA_G
    kb = min(max(r0 - NA_WIN_R // 2, 0), rows - NA_KR)
    pat = []
    for i in range(NA_G):
        rs = _na_row_start(r0 + i, rows)
        pat.append(tuple((kb + j - (r0 + i) + NA_WIN_R - 1) if rs <= kb + j < rs + NA_WIN_R else None
                         for j in range(NA_KR)))
    return r0, kb, tuple(pat)


def _na_bias_body(rpb_ref, o_ref, *, pattern):
    h = pl.program_id(0)
    n_dc = 2 * NA_WIN_C - 1
    base = h * ((2 * NA_WIN_R - 1) * n_dc)
    cc = lax.broadcasted_iota(jnp.int32, (GRID_W, LANES), 0)
    lane = lax.broadcasted_iota(jnp.int32, (GRID_W, LANES), 1)
    c = lane % GRID_W
    left = lane < GRID_W
    cs = jnp.clip(c - NA_WIN_C // 2, 0, GRID_W - NA_WIN_C)
    col_ok = (cc >= cs) & (cc < cs + NA_WIN_C)
    dcol = cc - c + NA_WIN_C - 1
    neg = jnp.full((GRID_W, LANES), NEG_BIG, F32)
    cache = {}

    def pair_tile(a0, a1):
        key = (a0, a1)
        if key in cache:
            return cache[key]
        t = neg
        for d in range(n_dc):
            v0 = rpb_ref[base + a0 * n_dc + d] * LOG2E if a0 is not None else NEG_BIG
            v1 = rpb_ref[base + a1 * n_dc + d] * LOG2E if a1 is not None else NEG_BIG
            val = jnp.where(left, v0, v1)
            t = jnp.where(col_ok & (dcol == d), val, t)
        cache[key] = t
        return t

    assert NA_G % 2 == 0
    for j in range(NA_KR):
        for ip in range(NA_G // 2):
            a0, a1 = pattern[2 * ip][j], pattern[2 * ip + 1][j]
            t = neg if (a0 is None and a1 is None) else pair_tile(a0, a1)
            o_ref[0, j * GRID_W:(j + 1) * GRID_W, ip * LANES:(ip + 1) * LANES] = t


def na_bias_tile(rpb_flat, pattern):
    tq, tk = NA_G * GRID_W, NA_KR * GRID_W
    return pl.pallas_call(
        functools.partial(_na_bias_body, pattern=pattern),
        grid=(NA_HEADS,),
        in_specs=[pl.BlockSpec(memory_space=pltpu.SMEM)],
        out_specs=pl.BlockSpec((1, tk, tq), lambda h: (h, 0, 0)),
        out_shape=jax.ShapeDtypeStruct((NA_HEADS, tk, tq), F32),
        compiler_params=_cp(("parallel",)),
        name="na_bias",
    )(rpb_flat)


def _na_attn_body(q_ref, k_ref, v_ref, *rest, rows):
    (bias_first, bias_mid, bias_last), o_ref = rest[:-1], rest[-1]
    tq, tk = NA_G * GRID_W, NA_KR * GRID_W
    n_grp = q_ref.shape[0] // tq
    bias_refs = [bias_first] + [bias_mid] * (n_grp - 2) + [bias_last]
    g0 = pl.program_id(2) * n_grp
    kws, vws, qms, biases = [], [], [], []
    for a in range(n_grp):
        kb = jnp.clip((g0 + a) * NA_G - NA_WIN_R // 2, 0, rows - NA_KR)
        start = pl.multiple_of(kb * GRID_W, GRID_W)
        kw = k_ref[pl.ds(start, tk), :]
        vw = v_ref[pl.ds(start, tk), :]
        q = q_ref[a * tq:(a + 1) * tq, :]
        first = lax.broadcasted_iota(jnp.int32, q.shape, 1) < NA_HEAD_DIM
        zero = jnp.zeros_like(q)
        for hh in range(2):
            kws.append(kw)
            vws.append(vw)
            qms.append(jnp.where(first if hh == 0 else jnp.logical_not(first), q, zero))
            biases.append(bias_refs[a][0, hh])
    sts = [_dot_nt(kw, qm) + b for kw, qm, b in zip(kws, qms, biases)]
    ms = [jnp.max(st, axis=0, keepdims=True) for st in sts]
    ps = [jnp.exp2(st - m) for st, m in zip(sts, ms)]
    ls = [jnp.sum(p, axis=0, keepdims=True) for p in ps]
    outs = [_dot_tn(vw, p.astype(BF16)) / l for vw, p, l in zip(vws, ps, ls)]
    top = lax.broadcasted_iota(jnp.int32, outs[0].shape, 0) < NA_HEAD_DIM
    for a in range(n_grp):
        o_ref[a * tq:(a + 1) * tq, :] = jnp.where(top, outs[2 * a], outs[2 * a + 1]).T.astype(BF16)


def na_attention(P, bias, B, L):
    rows = L // GRID_W
    ng = rows // NA_G
    n_grp = max(d for d in range(2, NA_GROUPS_PER_STEP + 1) if ng % d == 0)
    ns = ng // n_grp
    tq, tk = NA_G * GRID_W, NA_KR * GRID_W
    hp_n = NA_HEADS // 2
    qb, kb_, vb = COL_NAQ // LANES, COL_NAK // LANES, COL_NAV // LANES
    bias_specs = [
        pl.BlockSpec((1, 2, tk, tq), lambda b, hp, s: (jnp.where(s == 0, 0, 1), hp, 0, 0)),
        pl.BlockSpec((1, 2, tk, tq), lambda b, hp, s: (1, hp, 0, 0)),
        pl.BlockSpec((1, 2, tk, tq), lambda b, hp, s: (jnp.where(s == ns - 1, 2, 1), hp, 0, 0)),
    ]

    return pl.pallas_call(
        functools.partial(_na_attn_body, rows=rows),
        grid=(B, hp_n, ns),
        in_specs=[
            pl.BlockSpec((n_grp * tq, LANES), lambda b, hp, s: (b * ns + s, qb + hp)),
            pl.BlockSpec((L, LANES), lambda b, hp, s: (b, kb_ + hp)),
            pl.BlockSpec((L, LANES), lambda b, hp, s: (b, vb + hp)),
        ] + bias_specs,
        out_specs=pl.BlockSpec((n_grp * tq, LANES), lambda b, hp, s: (b * ns + s, hp)),
        out_shape=jax.ShapeDtypeStruct((B * L, NA_HEADS * NA_HEAD_DIM), BF16),
        compiler_params=_cp(("parallel", "parallel", "arbitrary")),
        name="na_attn",
    )(P, P, P, bias, bias, bias)


def _mem_attn_body(q_ref, kv_ref, o_ref):
    hd = MEM_HEAD_DIM
    heads = range(MEM_HEADS)
    ss = [_dot_nt(q_ref[:, h * hd:(h + 1) * hd], kv_ref[:, h * hd:(h + 1) * hd]) * (hd ** -0.5) for h in heads]
    ms = [jnp.max(s, axis=-1, keepdims=True) for s in ss]
    ps = [jnp.exp(s - m) for s, m in zip(ss, ms)]
    ls = [jnp.sum(p, axis=-1, keepdims=True) for p in ps]
    outs = [_dot(ps[h].astype(BF16), kv_ref[:, (MEM_HEADS + h) * hd:(MEM_HEADS + h + 1) * hd]) / ls[h]
            for h in heads]
    o_ref[...] = jnp.concatenate(outs, axis=-1).astype(BF16)


def mem_attention(P, kv, B, L, n_mem, tq=1024):
    W = MEM_HEADS * MEM_HEAD_DIM
    nt = L // tq
    return pl.pallas_call(
        _mem_attn_body,
        grid=(B, nt),
        in_specs=[
            pl.BlockSpec((tq, W), lambda b, t: (b * nt + t, COL_MQ // W)),
            pl.BlockSpec((n_mem, 2 * W), lambda b, t: (b, 0)),
        ],
        out_specs=pl.BlockSpec((tq, W), lambda b, t: (b * nt + t, 0)),
        out_shape=jax.ShapeDtypeStruct((B * L, W), BF16),
        compiler_params=_cp(("parallel", "parallel")),
        name="mem_attn",
    )(P, kv)


def _log_sigmoid(x):
    return jnp.minimum(x, 0.0) - jnp.log(1.0 + jnp.exp(-jnp.abs(x)))


def _gla_body(*refs, reverse, combine):
    if combine:
        q_ref, k_ref, v_ref, lr_ref, up_ref, gb_ref, ob_ref, ng_ref, o_ref, st_ref = refs
    else:
        q_ref, k_ref, v_ref, lr_ref, up_ref, gb_ref, o_ref, st_ref = refs
    T, C = GLA_T, GLA_CHUNK
    n_sub = T // C
    HK = GLA_HEADS * GLA_DK
    HV = GLA_HEADS * GLA_DV

    @pl.when(pl.program_id(1) == 0)
    def _():
        st_ref[...] = jnp.zeros_like(st_ref)

    r = lax.broadcasted_iota(jnp.int32, (T, T), 0)
    s = lax.broadcasted_iota(jnp.int32, (T, T), 1)
    same = (r // C) == (s // C)
    if reverse:
        csum_mask = same & (s >= r)
        att_mask = same & (s > r)
    else:
        csum_mask = same & (s <= r)
        att_mask = same & (s <= r)
    tri = jnp.where(csum_mask, 1.0, 0.0).astype(BF16)
    lane = lax.broadcasted_iota(jnp.int32, (T, HK), 1)
    er = lax.broadcasted_iota(jnp.int32, (HV, HK), 0) // GLA_DV
    dc = lax.broadcasted_iota(jnp.int32, (HV, HK), 1) // GLA_DK
    bd = er == dc
    edge_row = 0 if reverse else C - 1
    mid_row = C // 2 if reverse else C // 2 - 1

    nbb = q_ref.shape[0]
    bbs = range(nbb)
    heads = range(GLA_HEADS)
    hms = [(lane // GLA_DK) == h for h in heads]
    las = [_log_sigmoid(_dot3(lr_ref[bb], up_ref[...]) + gb_ref[...]) * (1.0 / GLA_TAU) for bb in bbs]
    splits = [_split_bf16(la) for la in las]
    bcs = [_dot(tri, hi) + _dot(tri, lo) for hi, lo in splits]
    edges = [jnp.concatenate([jnp.broadcast_to(bc[i * C + edge_row:i * C + edge_row + 1], (C, HK))
                              for i in range(n_sub)], axis=0) for bc in bcs]
    mids = [jnp.concatenate([jnp.broadcast_to(bc[i * C + mid_row:i * C + mid_row + 1], (C, HK))
                             for i in range(n_sub)], axis=0) for bc in bcs]
    qs = [q_ref[bb].astype(F32) * (GLA_DK ** -0.5) for bb in bbs]
    ks = [k_ref[bb].astype(F32) for bb in bbs]
    vs = [v_ref[bb] for bb in bbs]
    qes = [(q * jnp.exp(bc - mid)).astype(BF16) for q, bc, mid in zip(qs, bcs, mids)]
    kes = [(k * jnp.exp(mid - bc)).astype(BF16) for k, bc, mid in zip(ks, bcs, mids)]
    atts = [[_dot_nt(jnp.where(hms[h], qes[bb], jnp.zeros_like(qes[bb])), kes[bb]) for h in heads] for bb in bbs]
    qbs = [(q * jnp.exp(bc)).astype(BF16) for q, bc in zip(qs, bcs)]
    kls = [(k * jnp.exp(edge - bc)).astype(BF16) for k, bc, edge in zip(ks, bcs, edges)]
    decs = [[jnp.exp(edge[i * C:i * C + 1]) for i in range(n_sub)] for edge in edges]
    atts = [[jnp.where(att_mask, atts[bb][h], 0.0).astype(BF16) for h in heads] for bb in bbs]
    us = [[_dot_tn(vs[bb][i * C:(i + 1) * C], kls[bb][i * C:(i + 1) * C]) for i in range(n_sub)] for bb in bbs]
    intras = [[_dot(atts[bb][h], vs[bb][:, h * GLA_DV:(h + 1) * GLA_DV]) for h in heads] for bb in bbs]

    sts = [st_ref[bb] for bb in bbs]
    inters = [[None] * n_sub for _ in bbs]
    order = range(n_sub - 1, -1, -1) if reverse else range(n_sub)
    for i in order:
        for bb in bbs:
            inters[bb][i] = _dot_nt(qbs[bb][i * C:(i + 1) * C], sts[bb].astype(BF16))
            sts[bb] = decs[bb][i] * sts[bb] + jnp.where(bd, us[bb][i], 0.0)
    for bb in bbs:
        st_ref[bb] = sts[bb]
        o = jnp.concatenate(intras[bb], axis=-1) + jnp.concatenate(inters[bb], axis=0)
        if combine:
            o = o + ob_ref[bb]
            outs = []
            for h in heads:
                oh = o[:, h * GLA_DV:(h + 1) * GLA_DV]
                outs.append(oh * lax.rsqrt(jnp.mean(oh * oh, axis=-1, keepdims=True) + RMS_EPS))
            o_ref[bb] = (jnp.concatenate(outs, axis=-1) * ng_ref[...]).astype(o_ref.dtype)
        else:
            o_ref[bb] = o


def gla_direction(P, LR, up_pad, gbias, B, L, *, reverse, o_other=None, norm_g=None):
    T = GLA_T
    nb = L // T
    nbb = GLA_BATCH_PER_STEP if B % GLA_BATCH_PER_STEP == 0 else 1
    HK, HV = GLA_HEADS * GLA_DK, GLA_HEADS * GLA_DV
    combine = o_other is not None
    P3 = P.reshape(B, L, P.shape[-1])
    LR3 = LR.reshape(B, L, LR_PAD)

    def tok(j):
        return nb - 1 - j if reverse else j

    seq = lambda w, col: pl.BlockSpec((nbb, T, w), lambda b, j: (b, tok(j), col))
    in_specs = [
        seq(HK, COL_GQ // HK), seq(HK, COL_GK // HK), seq(HV, COL_GV // HV), seq(LR_PAD, 0),
        pl.BlockSpec((LR_PAD, HK), lambda b, j: (0, 0)),
        pl.BlockSpec((1, HK), lambda b, j: (0, 0)),
    ]
    args = [P3, P3, P3, LR3, up_pad, gbias]
    if combine:
        in_specs += [seq(HV, 0), pl.BlockSpec((1, HV), lambda b, j: (0, 0))]
        args += [o_other.reshape(B, L, HV), norm_g]
    out = pl.pallas_call(
        functools.partial(_gla_body, reverse=reverse, combine=combine),
        grid=(B // nbb, nb),
        in_specs=in_specs,
        out_specs=seq(HV, 0),
        out_shape=jax.ShapeDtypeStruct((B, L, HV), BF16 if combine else F32),
        scratch_shapes=[pltpu.VMEM((nbb, HV, HK), F32)],
        compiler_params=_cp(("parallel", "arbitrary")),
        name="gla_fwd" if combine else "gla_bwd",
    )(*args)
    return out.reshape(B * L, HV)


def _hy_conv_body(v_ref, x0_ref, x1_ref, wv_ref, wx0_ref, wx1_ref, bv_ref, bx0_ref, bx1_ref,
                  u_ref, x0o_ref, *, L, R):
    n_chunks = L // R
    HALO = 16
    rid = lax.broadcasted_iota(jnp.int32, (R, LANES), 0)

    def conv(ref, w_ref, b_ref, i):
        base = pl.multiple_of(i * R, R)
        cur = ref[pl.ds(base, R), :].astype(F32)
        pstart = pl.multiple_of(jnp.maximum(base - HALO, 0), HALO)
        nstart = pl.multiple_of(jnp.minimum(base + R, L - HALO), HALO)
        prev_row = ref[pl.ds(pstart, HALO), :][HALO - 1:HALO].astype(F32) * jnp.where(i > 0, 1.0, 0.0)
        next_row = ref[pl.ds(nstart, HALO), :][0:1].astype(F32) * jnp.where(i < n_chunks - 1, 1.0, 0.0)
        up = jnp.where(rid == 0, prev_row, pltpu.roll(cur, 1, 0))
        dn = jnp.where(rid == R - 1, next_row, pltpu.roll(cur, R - 1, 0))
        w = w_ref[...]
        return b_ref[...] + up * w[0:1] + cur * w[1:2] + dn * w[2:3]

    def step(i, carry):
        base = pl.multiple_of(i * R, R)
        cv = conv(v_ref, wv_ref, bv_ref, i)
        cx1 = conv(x1_ref, wx1_ref, bx1_ref, i)
        u_ref[pl.ds(base, R), :] = (cx1 * cv).astype(BF16)
        x0o_ref[pl.ds(base, R), :] = conv(x0_ref, wx0_ref, bx0_ref, i).astype(BF16)
        return carry

    lax.fori_loop(0, n_chunks, step, 0)


def hy_short_conv(P, conv_w, conv_b, B, L, R=1024):
    nc = HY_W // LANES
    bv, bx0, bx1 = COL_HYV // LANES, COL_HYX0 // LANES, COL_HYX1 // LANES
    seq = lambda off: pl.BlockSpec((L, LANES), lambda b, c: (b, off + c))
    wsp = lambda seg: pl.BlockSpec((HY_CONV, LANES), lambda b, c: (0, seg * nc + c))
    bsp = lambda seg: pl.BlockSpec((1, LANES), lambda b, c: (0, seg * nc + c))
    out = pl.BlockSpec((L, LANES), lambda b, c: (b, c))
    return pl.pallas_call(
        functools.partial(_hy_conv_body, L=L, R=min(R, L)),
        grid=(B, nc),
        in_specs=[seq(bv), seq(bx0), seq(bx1), wsp(0), wsp(1), wsp(2), bsp(0), bsp(1), bsp(2)],
        out_specs=[out, out],
        out_shape=[jax.ShapeDtypeStruct((B * L, HY_W), BF16)] * 2,
        compiler_params=_cp(("parallel", "parallel")),
        name="hy_conv",
    )(P, P, P, conv_w, conv_w, conv_w, conv_b, conv_b, conv_b)


def _hy_filter_body(freq_ref, delta_ref, w1_ref, b1_ref, wm_ref, bm_ref, w3_ref, k_ref, e_ref, *, L, R):
    i = pl.program_id(0)
    t = i * R + lax.broadcasted_iota(jnp.int32, (R, 1), 0)
    back = t >= L
    pos = jnp.where(back, 2 * L - t, t).astype(F32)
    t01 = pos / float(L - 1)
    lane = lax.broadcasted_iota(jnp.int32, (R, LANES), 1)
    ang = ((2.0 * math.pi / L) * pos) * freq_ref[...]
    z = jnp.where(lane == 0, t01,
                  jnp.where(lane <= HY_EMB_BANDS, jnp.cos(ang),
                            jnp.where(lane < HY_EMB, -jnp.sin(ang), 0.0)))
    h = jnp.sin(HY_SIN_FREQ * (_dot3(z, w1_ref[...]) + b1_ref[...]))
    for j in range(HY_INNER_MLPS):
        h = jnp.sin(HY_SIN_FREQ * (_dot3(h, wm_ref[j]) + bm_ref[j]))
    hw = _dot3(h, w3_ref[...])
    window = jnp.exp(-t01 * delta_ref[...])
    taps = jnp.where(back, hw[:, HY_W:], hw[:, :HY_W]) * window
    taps = jnp.where(t == L, 0.0, taps)
    k_ref[...] = taps.astype(BF16)

    @pl.when(i == 0)
    def _():
        e_ref[...] = jnp.zeros_like(e_ref)

    e_ref[...] += jnp.sum(taps * taps, axis=0, keepdims=True)


def hy_filter(freq, delta, w1p, b1p, wmp, bmp, w3p, L, R=1024):
    R = min(R, L)
    full = lambda a: pl.BlockSpec(a.shape, lambda i, _n=a.ndim: (0,) * _n)
    args = (freq, delta, w1p, b1p, wmp, bmp, w3p)
    return pl.pallas_call(
        functools.partial(_hy_filter_body, L=L, R=R),
        grid=(2 * L // R,),
        in_specs=[full(a) for a in args],
        out_specs=[pl.BlockSpec((R, HY_W), lambda i: (i, 0)), pl.BlockSpec((1, HY_W), lambda i: (0, 0))],
        out_shape=[jax.ShapeDtypeStruct((2 * L, HY_W), BF16), jax.ShapeDtypeStruct((1, HY_W), F32)],
        compiler_params=_cp(("arbitrary",)),
        name="hy_filter",
    )(*args)


def _dft_slow_body(c_ref, x_ref, o_ref):
    two, half, t2n, w = x_ref.shape
    x = jnp.swapaxes(x_ref[...].reshape(two * half, t2n, w), 0, 1)
    c = c_ref[...]
    y = jnp.stack([_dot(c, x[t]).astype(BF16) for t in range(t2n)], axis=0)
    o_ref[...] = jnp.swapaxes(y, 0, 1).reshape(o_ref.shape)


def dft_slow(cmat, x5, t2n=DFT_T2_BLOCK):
    R, K = cmat.shape
    npair, _, half, n2, w = x5.shape
    assert K == 2 * half and n2 % t2n == 0
    return pl.pallas_call(
        _dft_slow_body,
        grid=(npair, n2 // t2n),
        in_specs=[
            pl.BlockSpec((R, K), lambda p, j: (0, 0)),
            pl.BlockSpec((None, 2, half, t2n, w), lambda p, j: (p, 0, 0, j, 0)),
        ],
        out_specs=pl.BlockSpec((None, 2, R // 2, t2n, w), lambda p, j: (p, 0, 0, j, 0)),
        out_shape=jax.ShapeDtypeStruct((npair, 2, R // 2, n2, w), BF16),
        compiler_params=_cp(("parallel", "parallel")),
        name="dft_slow",
    )(cmat, x5)


def _stack_complex(tab_ref, k):
    cr, ci = tab_ref[k, 0], tab_ref[k, 1]
    return jnp.concatenate([jnp.concatenate([cr, -ci], axis=1), jnp.concatenate([ci, cr], axis=1)], axis=0)


def _hy_kf_body(a_ref, cf_ref, e_ref, o_ref, *, F, n_fft):
    n2 = DFT_N2
    scale = lax.rsqrt(e_ref[...] + 1e-6) * (1.0 / n_fft)
    for k in range(F):
        a = jnp.concatenate([a_ref[0, k], a_ref[1, k]], axis=0)
        x = _dot(_stack_complex(cf_ref, k), a) * scale
        o_ref[k, 0] = x[:n2]
        o_ref[k, 1] = x[n2:]


def hy_filter_spectrum(a5, cf_tab, energy, n_fft, F=8):
    n1, n2 = n_fft // DFT_N2, DFT_N2
    F = min(F, n1)
    return pl.pallas_call(
        functools.partial(_hy_kf_body, F=F, n_fft=n_fft),
        grid=(n1 // F,),
        in_specs=[
            pl.BlockSpec((2, F, n2, HY_W), lambda f: (0, f, 0, 0)),
            pl.BlockSpec((F, 2, n2, n2), lambda f: (f, 0, 0, 0)),
            pl.BlockSpec((1, HY_W), lambda f: (0, 0)),
        ],
        out_specs=pl.BlockSpec((F, 2, n2, HY_W), lambda f: (f, 0, 0, 0)),
        out_shape=jax.ShapeDtypeStruct((n1, 2, n2, HY_W), F32),
        compiler_params=_cp(("parallel",)),
        name="hy_kf",
    )(a5, cf_tab, energy)


def _hy_mid_body(a_ref, kf_ref, cf_ref, ci_ref, o_ref, *, F):
    n2 = DFT_N2
    ks = range(F)
    xs = [_dot(_stack_complex(cf_ref, k), jnp.concatenate([a_ref[0, k], a_ref[1, k]], axis=0)) for k in ks]
    ys = []
    for k in ks:
        xr, xi = xs[k][:n2], xs[k][n2:]
        kr, ki = kf_ref[k, 0], kf_ref[k, 1]
        ys.append(jnp.concatenate([xr * kr - xi * ki, xr * ki + xi * kr], axis=0).astype(BF16))
    bvs = [_dot(_stack_complex(ci_ref, k), ys[k]) for k in ks]
    for k in ks:
        o_ref[0, k] = bvs[k][:n2].astype(BF16)
        o_ref[1, k] = bvs[k][n2:].astype(BF16)


def hy_mid(a5, kf, cf_tab, ci_tab, F=8):
    npair = a5.shape[0]
    n1, n2 = kf.shape[0], DFT_N2
    F = min(F, n1)
    return pl.pallas_call(
        functools.partial(_hy_mid_body, F=F),
        grid=(n1 // F, npair),
        in_specs=[
            pl.BlockSpec((None, 2, F, n2, HY_W), lambda f, p: (p, 0, f, 0, 0)),
            pl.BlockSpec((F, 2, n2, HY_W), lambda f, p: (f, 0, 0, 0)),
            pl.BlockSpec((F, 2, n2, n2), lambda f, p: (f, 0, 0, 0)),
            pl.BlockSpec((F, 2, n2, n2), lambda f, p: (f, 0, 0, 0)),
        ],
        out_specs=pl.BlockSpec((None, 2, F, n2, HY_W), lambda f, p: (p, 0, f, 0, 0)),
        out_shape=jax.ShapeDtypeStruct((npair, 2, n1, n2, HY_W), BF16),
        compiler_params=_cp(("parallel", "arbitrary")),
        name="hy_mid",
    )(a5, kf, cf_tab, ci_tab)


def _hy_out_body(c_ref, b_ref, u_ref, x0_ref, skip_ref, o_ref):
    two, n1, t2n, w = b_ref.shape
    b = jnp.swapaxes(b_ref[...].reshape(two * n1, t2n, w), 0, 1)
    c = c_ref[...]
    y = jnp.stack([_dot(c, b[t]) for t in range(t2n)], axis=0)
    y = jnp.swapaxes(y, 0, 1).reshape(o_ref.shape)
    conv = y + skip_ref[...] * u_ref[...].astype(F32)
    o_ref[...] = (x0_ref[...].astype(F32) * conv).astype(BF16)


def hy_out(cmat, bsp, u5, x05, skip, t2n=DFT_T2_BLOCK):
    npair, _, n1, n2, w = bsp.shape
    half = u5.shape[2]
    pair = pl.BlockSpec((None, 2, half, t2n, w), lambda p, j: (p, 0, 0, j, 0))
    return pl.pallas_call(
        _hy_out_body,
        grid=(npair, n2 // t2n),
        in_specs=[
            pl.BlockSpec((2 * half, 2 * n1), lambda p, j: (0, 0)),
            pl.BlockSpec((None, 2, n1, t2n, w), lambda p, j: (p, 0, 0, j, 0)),
            pair, pair,
            pl.BlockSpec((1, w), lambda p, j: (0, 0)),
        ],
        out_specs=pair,
        out_shape=jax.ShapeDtypeStruct((npair, 2, half, n2, w), BF16),
        compiler_params=_cp(("parallel", "parallel")),
        name="hy_out",
    )(cmat, bsp, u5, x05, skip)


@functools.lru_cache(maxsize=None)
def _dft_tables(L):
    n = 2 * L
    n2 = DFT_N2
    n1 = n // n2
    hl = n1 // 2
    f1 = np.arange(n1)
    t1 = np.arange(n1)
    th1 = 2.0 * np.pi * ((f1[:, None] * t1[None, :]) % n1) / n1
    fr, fi = np.cos(th1), -np.sin(th1)
    c1_data = np.block([[fr[:, :hl], -fi[:, :hl]], [fi[:, :hl], fr[:, :hl]]])
    c1_real = np.concatenate([fr, fi], axis=0)
    gr, gi = np.cos(th1.T)[:hl], np.sin(th1.T)[:hl]
    c1_inv = np.block([[gr, -gi], [gi, gr]])
    f2 = np.arange(n2)
    t2 = np.arange(n2)
    k = (t2[None, None, :] * (f1[:, None, None] + n1 * f2[None, :, None])) % n
    th = 2.0 * np.pi * k / n
    cf = np.stack([np.cos(th), -np.sin(th)], axis=1)
    tht = np.transpose(th, (0, 2, 1))
    ci = np.stack([np.cos(tht), np.sin(tht)], axis=1)
    return tuple(a.astype(np.float32).astype(BF16) for a in (c1_data, c1_real, c1_inv, cf, ci))


def hyena_branch(P, B, L, conv_w, conv_b, w1, b1, wm, bm, w3, skip):
    assert B % 2 == 0 and (2 * L) % DFT_N2 == 0 and L % DFT_N2 == 0
    n_fft = 2 * L
    n1, n2 = n_fft // DFT_N2, DFT_N2
    hl = n1 // 2
    NL = n2 * HY_W
    c1_data, c1_real, c1_inv, cf_tab, ci_tab = (jnp.asarray(a) for a in _dft_tables(L))

    hid = LANES
    freqs = np.linspace(1e-4, HY_EMB_BANDS - 1, HY_EMB_BANDS, dtype=np.float32)
    freq_lane = np.zeros((1, LANES), np.float32)
    freq_lane[0, 1:1 + HY_EMB_BANDS] = freqs
    freq_lane[0, 1 + HY_EMB_BANDS:HY_EMB] = freqs
    delta = np.abs(np.linspace(HY_MIN_DECAY, HY_MAX_DECAY, HY_W, dtype=np.float32)).reshape(1, HY_W)
    w1p = jnp.zeros((LANES, hid), F32).at[:HY_EMB, :HY_FILTER_HID].set(w1)
    b1p = jnp.zeros((1, hid), F32).at[0, :HY_FILTER_HID].set(b1)
    wmp = jnp.zeros((HY_INNER_MLPS, hid, hid), F32).at[:, :HY_FILTER_HID, :HY_FILTER_HID].set(wm)
    bmp = jnp.zeros((HY_INNER_MLPS, 1, hid), F32).at[:, 0, :HY_FILTER_HID].set(bm)
    w3p = jnp.zeros((hid, 2 * HY_W), F32).at[:HY_FILTER_HID].set(w3)
    kern, energy = hy_filter(jnp.asarray(freq_lane), jnp.asarray(delta), w1p, b1p, wmp, bmp, w3p, L)
    ak = dft_slow(c1_real, kern.reshape(1, 2, hl, n2, HY_W))[0]
    kf = hy_filter_spectrum(ak, cf_tab, energy, n_fft)

    u, x0c = hy_short_conv(P, conv_w, conv_b.reshape(1, -1), B, L)
    u5 = u.reshape(B // 2, 2, hl, n2, HY_W)
    x05 = x0c.reshape(B // 2, 2, hl, n2, HY_W)
    a = dft_slow(c1_data, u5)
    bsp = hy_mid(a, kf, cf_tab, ci_tab)
    return hy_out(c1_inv, bsp, u5, x05, skip.reshape(1, HY_W)).reshape(B * L, HY_W)


def _merge_body(ona_ref, ohy_ref, ogl_ref, omem_ref, z_ref, g_ref, wbr_ref, wout_ref, bout_ref,
                x_ref, lg_ref, lb_ref, h32_ref, *maybe_h16_ref, alpha):
    D = x_ref.shape[-1]
    branches = range(N_BRANCH)
    o_refs = (ona_ref, ohy_ref, ogl_ref, omem_ref)
    zs = [z_ref[:, j * BRANCH_W:(j + 1) * BRANCH_W] for j in branches]
    acts = [o_refs[j][...] * (zs[j] * (1.0 + jnp.tanh(zs[j]))) for j in branches]
    projs = [_dot(acts[j], wbr_ref[j]) for j in branches]
    gates = [1.0 + jnp.tanh(g_ref[:, j * D:(j + 1) * D].astype(F32)) for j in branches]
    merged = gates[0] * projs[0]
    for j in branches[1:]:
        merged = merged + gates[j] * projs[j]
    out = _dot(merged.astype(BF16), wout_ref[...]) + bout_ref[...]
    h = _layer_norm_rows(alpha * x_ref[...] + out, lg_ref[...], lb_ref[...])
    h32_ref[...] = h
    for h16_ref in maybe_h16_ref:
        h16_ref[...] = h.astype(BF16)


def merge_layer(o_na, o_hy, o_gla, o_mem, P, wbr, wout, bout, x32, lg, lb, alpha, want_bf16, tm=512):
    M, D = x32.shape
    row = lambda w, blk=0: pl.BlockSpec((tm, w), lambda i, _b=blk: (i, _b))
    vec = pl.BlockSpec((1, D), lambda i: (0, 0))
    n_out = 2 if want_bf16 else 1
    outs = pl.pallas_call(
        functools.partial(_merge_body, alpha=alpha),
        grid=(M // tm,),
        in_specs=[
            row(BRANCH_W), row(BRANCH_W), row(BRANCH_W), row(BRANCH_W),
            row(N_BRANCH * BRANCH_W, COL_Z // (N_BRANCH * BRANCH_W)),
            row(N_BRANCH * D, COL_G // (N_BRANCH * D)),
            pl.BlockSpec((N_BRANCH, BRANCH_W, D), lambda i: (0, 0, 0), pipeline_mode=pl.Buffered(1)),
            pl.BlockSpec((D, D), lambda i: (0, 0), pipeline_mode=pl.Buffered(1)),
            vec, row(D), vec, vec,
        ],
        out_specs=[row(D), row(D)][:n_out],
        out_shape=[jax.ShapeDtypeStruct((M, D), F32), jax.ShapeDtypeStruct((M, D), BF16)][:n_out],
        compiler_params=_cp(("parallel",)),
        name="merge",
    )(o_na, o_hy, o_gla, o_mem, P, P, wbr, wout, bout, x32, lg, lb)
    return (outs[0], outs[1]) if want_bf16 else (outs[0], None)


def _permute_in_proj(w_in, b_in):
    sizes = (512, 512, 512, 1536, 256, 256, 512, GLA_RANK, GLA_RANK, 512, 2048, 4096)
    offs = np.concatenate([[0], np.cumsum(sizes)])
    seg = lambda a, i: a[..., offs[i]:offs[i + 1]]
    order = (11, 10, 0, 1, 2, 3, 4, 5, 6, 9)
    col_scale = {0: NA_QSCALE, 10: 0.5, 11: 0.5}
    scaled = lambda a, i: seg(a, i) * col_scale[i] if i in col_scale else seg(a, i)
    wm = jnp.concatenate([scaled(w_in, i) for i in order], axis=-1).astype(BF16)
    bm = jnp.concatenate([scaled(b_in, i) for i in order], axis=-1).reshape(1, -1)
    pad = LR_PAD - 2 * GLA_RANK
    wlr = jnp.pad(jnp.concatenate([seg(w_in, 7), seg(w_in, 8)], axis=-1), ((0, 0), (0, pad))).astype(BF16)
    blr = jnp.pad(jnp.concatenate([seg(b_in, 7), seg(b_in, 8)], axis=-1), ((0, pad),)).reshape(1, -1)
    return wm, bm, wlr, blr


def _hybrid_layer(h32, h16, mem16, B, L, depth, want_bf16, w_in, b_in, na_rpb, hy_conv_w, hy_conv_b, hy_w1, hy_b1,
                  hy_wm, hy_bm, hy_w3, hy_skip, gla_up_f, gla_bias_f, gla_up_b, gla_bias_b, gla_norm_g,
                  mem_w_kv, w_br, w_out, b_out, ln_g, ln_b):
    D = h32.shape[-1]
    rows = L // GRID_W
    assert rows % NA_G == 0 and rows >= NA_KR and L % GLA_T == 0
    wm, bm, wlr, blr = _permute_in_proj(w_in, b_in)
    assert wm.shape[1] == N_MAIN
    P, LR = in_proj(h16, wm, bm, wlr, blr)

    ng = rows // NA_G
    geo = [_na_group_geometry(g, rows) for g in range(ng)]
    assert all(geo[g][2] == geo[1][2] for g in range(1, ng - 1))
    rpb_flat = na_rpb.reshape(-1)
    bias = jnp.stack([na_bias_tile(rpb_flat, geo[g][2]) for g in (0, 1, ng - 1)], axis=0)
    o_na = na_attention(P, bias, B, L)

    o_hy = hyena_branch(P, B, L, hy_conv_w, hy_conv_b, hy_w1, hy_b1, hy_wm, hy_bm, hy_w3, hy_skip)

    HK = GLA_HEADS * GLA_DK
    up_f = jnp.zeros((LR_PAD, HK), F32).at[:GLA_RANK].set(gla_up_f)
    up_b = jnp.zeros((LR_PAD, HK), F32).at[GLA_RANK:2 * GLA_RANK].set(gla_up_b)
    o_b = gla_direction(P, LR, up_b, gla_bias_b.reshape(1, HK), B, L, reverse=True)
    o_gla = gla_direction(P, LR, up_f, gla_bias_f.reshape(1, HK), B, L, reverse=False,
                          o_other=o_b, norm_g=gla_norm_g.reshape(1, -1))

    n_mem = mem16.shape[0] // B
    kv = matmul_bf16(mem16, mem_w_kv.astype(BF16))
    o_mem = mem_attention(P, kv, B, L, n_mem)

    alpha = (2 * depth) ** 0.25
    return merge_layer(o_na, o_hy, o_gla, o_mem, P, (0.5 * w_br).astype(BF16), w_out.astype(BF16),
                       b_out.reshape(1, D), h32, ln_g.reshape(1, D), ln_b.reshape(1, D), alpha, want_bf16)


def kernel(x, mem, ln_in_g, ln_in_b, w_in, b_in, na_rpb, hy_conv_w, hy_conv_b, hy_w1, hy_b1, hy_wm, hy_bm,
           hy_w3, hy_skip, gla_up_f, gla_bias_f, gla_up_b, gla_bias_b, gla_norm_g, mem_w_kv, w_br, w_out,
           b_out, ln_g, ln_b):
    B, L, D = x.shape
    depth = w_in.shape[0]
    h32, h16 = ln_in(x.reshape(B * L, D), ln_in_g, ln_in_b)
    mem16 = mem.reshape(-1, D).astype(BF16)
    for i in range(depth):
        h32, h16 = _hybrid_layer(
            h32, h16, mem16, B, L, depth, i + 1 < depth, w_in[i], b_in[i], na_rpb[i], hy_conv_w[i], hy_conv_b[i], hy_w1[i],
            hy_b1[i], hy_wm[i], hy_bm[i], hy_w3[i], hy_skip[i], gla_up_f[i], gla_bias_f[i], gla_up_b[i],
            gla_bias_b[i], gla_norm_g[i], mem_w_kv[i], w_br[i], w_out[i], b_out[i], ln_g[i], ln_b[i])
    return h32.reshape(B, L, D)
```

```python
import functools
import math

import numpy as np
import jax
import jax.numpy as jnp
from jax import lax
from jax.experimental import pallas as pl
from jax.experimental.pallas import tpu as pltpu

F32 = jnp.float32
BF16 = jnp.bfloat16

GRID_W = 64
NA_HEADS, NA_HEAD_DIM, NA_WIN_R, NA_WIN_C = 8, 64, 8, 16
HY_W, HY_CONV, HY_EMB_BANDS, HY_FILTER_HID, HY_INNER_MLPS = 512, 3, 16, 64, 2
HY_EMB = 1 + 2 * HY_EMB_BANDS
HY_SIN_FREQ = 1.0
HY_DECAY_TARGET, HY_FAST_DECAY_PCT, HY_SLOW_DECAY_PCT = 1e-2, 0.3, 1.5
HY_MAX_DECAY = math.log(HY_DECAY_TARGET) / HY_FAST_DECAY_PCT
HY_MIN_DECAY = math.log(HY_DECAY_TARGET) / HY_SLOW_DECAY_PCT
GLA_HEADS, GLA_DK, GLA_DV, GLA_RANK, GLA_TAU, GLA_CHUNK = 4, 64, 128, 16, 16.0, 64
MEM_HEADS, MEM_HEAD_DIM = 4, 128
N_BRANCH, BRANCH_W = 4, 512
LN_EPS, RMS_EPS = 1e-5, 1e-6

LANES = 128
VMEM_LIMIT = 52 * 1024 * 1024
NEG_BIG = -1e30

COL_G = 0
COL_Z = 4096
COL_NAQ, COL_NAK, COL_NAV = 6144, 6656, 7168
COL_HYV, COL_HYX0, COL_HYX1 = 7680, 8192, 8704
COL_GQ, COL_GK, COL_GV = 9216, 9472, 9728
COL_MQ = 10240
N_MAIN = 10752
LR_PAD = 128

LOG2E = math.log2(math.e)
NA_QSCALE = NA_HEAD_DIM ** -0.5 * LOG2E
NA_G = 4
NA_KR = NA_G + NA_WIN_R - 1
NA_GROUPS_PER_STEP = 8
GLA_T = 256
GLA_BATCH_PER_STEP = 8
DFT_N2 = 128
DFT_T2_BLOCK = 16


def _cp(sem):
    return pltpu.CompilerParams(dimension_semantics=sem, vmem_limit_bytes=VMEM_LIMIT)


def _split_bf16(a):
    hi = a.astype(BF16)
    lo = (a - hi.astype(F32)).astype(BF16)
    return hi, lo


def _dot(a, b):
    return jnp.dot(a, b, preferred_element_type=F32)


def _dot3(a, b):
    ah, al = _split_bf16(a)
    bh, bl = _split_bf16(b)
    return _dot(ah, bh) + _dot(al, bh) + _dot(ah, bl)


def _dot_nt(a, b):
    return lax.dot_general(a, b, (((1,), (1,)), ((), ())), preferred_element_type=F32)


def _dot_tn(a, b):
    return lax.dot_general(a, b, (((0,), (0,)), ((), ())), preferred_element_type=F32)


def _layer_norm_rows(y, g, b):
    mu = jnp.mean(y, axis=-1, keepdims=True)
    d = y - mu
    var = jnp.mean(d * d, axis=-1, keepdims=True)
    return d * lax.rsqrt(var + LN_EPS) * g + b


def _ln_in_body(x_ref, g_ref, b_ref, h32_ref, h16_ref):
    h = _layer_norm_rows(x_ref[...], g_ref[...], b_ref[...])
    h32_ref[...] = h
    h16_ref[...] = h.astype(BF16)


def ln_in(x2, g, b, tm=1024):
    M, D = x2.shape
    row = pl.BlockSpec((tm, D), lambda i: (i, 0))
    vec = pl.BlockSpec((1, D), lambda i: (0, 0))
    return pl.pallas_call(
        _ln_in_body,
        grid=(M // tm,),
        in_specs=[row, vec, vec],
        out_specs=[row, row],
        out_shape=[jax.ShapeDtypeStruct((M, D), F32), jax.ShapeDtypeStruct((M, D), BF16)],
        compiler_params=_cp(("parallel",)),
        name="ln_in",
    )(x2, g.reshape(1, D), b.reshape(1, D))


def _in_proj_body(x_ref, w_ref, b_ref, wlr_ref, blr_ref, p_ref, lr_ref):
    x = x_ref[...]
    p_ref[...] = (_dot(x, w_ref[...]) + b_ref[...]).astype(BF16)

    @pl.when(pl.program_id(1) == 0)
    def _():
        lr_ref[...] = _dot(x, wlr_ref[...]) + blr_ref[...]


def in_proj(h16, wm, bm, wlr, blr, tm=2048, tn=1536):
    M, D = h16.shape
    NM = wm.shape[1]
    return pl.pallas_call(
        _in_proj_body,
        grid=(M // tm, NM // tn),
        in_specs=[
            pl.BlockSpec((tm, D), lambda i, j: (i, 0)),
            pl.BlockSpec((D, tn), lambda i, j: (0, j)),
            pl.BlockSpec((1, tn), lambda i, j: (0, j)),
            pl.BlockSpec((D, LR_PAD), lambda i, j: (0, 0)),
            pl.BlockSpec((1, LR_PAD), lambda i, j: (0, 0)),
        ],
        out_specs=[
            pl.BlockSpec((tm, tn), lambda i, j: (i, j)),
            pl.BlockSpec((tm, LR_PAD), lambda i, j: (i, 0)),
        ],
        out_shape=[jax.ShapeDtypeStruct((M, NM), BF16), jax.ShapeDtypeStruct((M, LR_PAD), F32)],
        compiler_params=_cp(("parallel", "arbitrary")),
        name="in_proj",
    )(h16, wm, bm, wlr, blr)


def _matmul_body(x_ref, w_ref, o_ref):
    o_ref[...] = _dot(x_ref[...], w_ref[...]).astype(o_ref.dtype)


def matmul_bf16(x, w, tm=512):
    M, K = x.shape
    N = w.shape[1]
    return pl.pallas_call(
        _matmul_body,
        grid=(M // tm,),
        in_specs=[pl.BlockSpec((tm, K), lambda i: (i, 0)), pl.BlockSpec((K, N), lambda i: (0, 0))],
        out_specs=pl.BlockSpec((tm, N), lambda i: (i, 0)),
        out_shape=jax.ShapeDtypeStruct((M, N), BF16),
        compiler_params=_cp(("parallel",)),
        name="mem_kv",
    )(x, w)


def _na_row_start(r, rows):
    return min(max(r - NA_WIN_R // 2, 0), rows - NA_WIN_R)


def _na_group_geometry(g, rows):
    r0 = g * NA_G
    kb = min(max(r0 - NA_WIN_R // 2, 0), rows - NA_KR)
    pat = []
    for i in range(NA_G):
        rs = _na_row_start(r0 + i, rows)
        pat.append(tuple((kb + j - (r0 + i) + NA_WIN_R - 1) if rs <= kb + j < rs + NA_WIN_R else None
                         for j in range(NA_KR)))
    return r0, kb, tuple(pat)


def _na_bias_body(rpb_ref, o_ref, *, pattern):
    h = pl.program_id(0)
    n_dc = 2 * NA_WIN_C - 1
    base = h * ((2 * NA_WIN_R - 1) * n_dc)
    cc = lax.broadcasted_iota(jnp.int32, (GRID_W, LANES), 0)
    lane = lax.broadcasted_iota(jnp.int32, (GRID_W, LANES), 1)
    c = lane % GRID_W
    left = lane < GRID_W
    cs = jnp.clip(c - NA_WIN_C // 2, 0, GRID_W - NA_WIN_C)
    col_ok = (cc >= cs) & (cc < cs + NA_WIN_C)
    dcol = cc - c + NA_WIN_C - 1
    neg = jnp.full((GRID_W, LANES), NEG_BIG, F32)
    cache = {}

    def pair_tile(a0, a1):
        key = (a0, a1)
        if key in cache:
            return cache[key]
        t = neg
        for d in range(n_dc):
            v0 = rpb_ref[base + a0 * n_dc + d] * LOG2E if a0 is not None else NEG_BIG
            v1 = rpb_ref[base + a1 * n_dc + d] * LOG2E if a1 is not None else NEG_BIG
            val = jnp.where(left, v0, v1)
            t = jnp.where(col_ok & (dcol == d), val, t)
        cache[key] = t
        return t

    assert NA_G % 2 == 0
    for j in range(NA_KR):
        for ip in range(NA_G // 2):
            a0, a1 = pattern[2 * ip][j], pattern[2 * ip + 1][j]
            t = neg if (a0 is None and a1 is None) else pair_tile(a0, a1)
            o_ref[0, j * GRID_W:(j + 1) * GRID_W, ip * LANES:(ip + 1) * LANES] = t


def na_bias_tile(rpb_flat, pattern):
    tq, tk = NA_G * GRID_W, NA_KR * GRID_W
    return pl.pallas_call(
        functools.partial(_na_bias_body, pattern=pattern),
        grid=(NA_HEADS,),
        in_specs=[pl.BlockSpec(memory_space=pltpu.SMEM)],
        out_specs=pl.BlockSpec((1, tk, tq), lambda h: (h, 0, 0)),
        out_shape=jax.ShapeDtypeStruct((NA_HEADS, tk, tq), F32),
        compiler_params=_cp(("parallel",)),
        name="na_bias",
    )(rpb_flat)


def _na_attn_body(q_ref, k_ref, v_ref, *rest, rows):
    (bias_first, bias_mid, bias_last), o_ref = rest[:-1], rest[-1]
    tq, tk = NA_G * GRID_W, NA_KR * GRID_W
    n_grp = q_ref.shape[0] // tq
    bias_refs = [bias_first] + [bias_mid] * (n_grp - 2) + [bias_last]
    g0 = pl.program_id(2) * n_grp
    kws, vws, qms, biases = [], [], [], []
    for a in range(n_grp):
        kb = jnp.clip((g0 + a) * NA_G - NA_WIN_R // 2, 0, rows - NA_KR)
        start = pl.multiple_of(kb * GRID_W, GRID_W)
        kw = k_ref[pl.ds(start, tk), :]
        vw = v_ref[pl.ds(start, tk), :]
        q = q_ref[a * tq:(a + 1) * tq, :]
        first = lax.broadcasted_iota(jnp.int32, q.shape, 1) < NA_HEAD_DIM
        zero = jnp.zeros_like(q)
        for hh in range(2):
            kws.append(kw)
            vws.append(vw)
            qms.append(jnp.where(first if hh == 0 else jnp.logical_not(first), q, zero))
            biases.append(bias_refs[a][0, hh])
    sts = [_dot_nt(kw, qm) + b for kw, qm, b in zip(kws, qms, biases)]
    ms = [jnp.max(st, axis=0, keepdims=True) for st in sts]
    ps = [jnp.exp2(st - m) for st, m in zip(sts, ms)]
    ls = [jnp.sum(p, axis=0, keepdims=True) for p in ps]
    outs = [_dot_tn(vw, p.astype(BF16)) / l for vw, p, l in zip(vws, ps, ls)]
    top = lax.broadcasted_iota(jnp.int32, outs[0].shape, 0) < NA_HEAD_DIM
    for a in range(n_grp):
        o_ref[a * tq:(a + 1) * tq, :] = jnp.where(top, outs[2 * a], outs[2 * a + 1]).T.astype(BF16)


def na_attention(P, bias, B, L):
    rows = L // GRID_W
    ng = rows // NA_G
    n_grp = max(d for d in range(2, NA_GROUPS_PER_STEP + 1) if ng % d == 0)
    ns = ng // n_grp
    tq, tk = NA_G * GRID_W, NA_KR * GRID_W
    hp_n = NA_HEADS // 2
    qb, kb_, vb = COL_NAQ // LANES, COL_NAK // LANES, COL_NAV // LANES
    bias_specs = [
        pl.BlockSpec((1, 2, tk, tq), lambda b, hp, s: (jnp.where(s == 0, 0, 1), hp, 0, 0)),
        pl.BlockSpec((1, 2, tk, tq), lambda b, hp, s: (1, hp, 0, 0)),
        pl.BlockSpec((1, 2, tk, tq), lambda b, hp, s: (jnp.where(s == ns - 1, 2, 1), hp, 0, 0)),
    ]

    return pl.pallas_call(
        functools.partial(_na_attn_body, rows=rows),
        grid=(B, hp_n, ns),
        in_specs=[
            pl.BlockSpec((n_grp * tq, LANES), lambda b, hp, s: (b * ns + s, qb + hp)),
            pl.BlockSpec((L, LANES), lambda b, hp, s: (b, kb_ + hp)),
            pl.BlockSpec((L, LANES), lambda b, hp, s: (b, vb + hp)),
        ] + bias_specs,
        out_specs=pl.BlockSpec((n_grp * tq, LANES), lambda b, hp, s: (b * ns + s, hp)),
        out_shape=jax.ShapeDtypeStruct((B * L, NA_HEADS * NA_HEAD_DIM), BF16),
        compiler_params=_cp(("parallel", "parallel", "arbitrary")),
        name="na_attn",
    )(P, P, P, bias, bias, bias)


def _mem_attn_body(q_ref, kv_ref, o_ref):
    hd = MEM_HEAD_DIM
    outs = []
    for h in range(MEM_HEADS):
        q = q_ref[:, h * hd:(h + 1) * hd]
        k = kv_ref[:, h * hd:(h + 1) * hd]
        v = kv_ref[:, (MEM_HEADS + h) * hd:(MEM_HEADS + h + 1) * hd]
        s = _dot_nt(q, k) * (hd ** -0.5)
        m = jnp.max(s, axis=-1, keepdims=True)
        p = jnp.exp(s - m)
        l = jnp.sum(p, axis=-1, keepdims=True)
        outs.append(_dot(p.astype(BF16), v) / l)
    o_ref[...] = jnp.concatenate(outs, axis=-1).astype(BF16)


def mem_attention(P, kv, B, L, n_mem, tq=1024):
    W = MEM_HEADS * MEM_HEAD_DIM
    nt = L // tq
    return pl.pallas_call(
        _mem_attn_body,
        grid=(B, nt),
        in_specs=[
            pl.BlockSpec((tq, W), lambda b, t: (b * nt + t, COL_MQ // W)),
            pl.BlockSpec((n_mem, 2 * W), lambda b, t: (b, 0)),
        ],
        out_specs=pl.BlockSpec((tq, W), lambda b, t: (b * nt + t, 0)),
        out_shape=jax.ShapeDtypeStruct((B * L, W), BF16),
        compiler_params=_cp(("parallel", "parallel")),
        name="mem_attn",
    )(P, kv)


def _log_sigmoid(x):
    return jnp.minimum(x, 0.0) - jnp.log(1.0 + jnp.exp(-jnp.abs(x)))


def _gla_body(*refs, reverse, combine):
    if combine:
        q_ref, k_ref, v_ref, lr_ref, up_ref, gb_ref, ob_ref, ng_ref, o_ref, st_ref = refs
    else:
        q_ref, k_ref, v_ref, lr_ref, up_ref, gb_ref, o_ref, st_ref = refs
    T, C = GLA_T, GLA_CHUNK
    n_sub = T // C
    HK = GLA_HEADS * GLA_DK
    HV = GLA_HEADS * GLA_DV

    @pl.when(pl.program_id(1) == 0)
    def _():
        st_ref[...] = jnp.zeros_like(st_ref)

    r = lax.broadcasted_iota(jnp.int32, (T, T), 0)
    s = lax.broadcasted_iota(jnp.int32, (T, T), 1)
    same = (r // C) == (s // C)
    if reverse:
        csum_mask = same & (s >= r)
        att_mask = same & (s > r)
    else:
        csum_mask = same & (s <= r)
        att_mask = same & (s <= r)
    tri = jnp.where(csum_mask, 1.0, 0.0).astype(BF16)
    lane = lax.broadcasted_iota(jnp.int32, (T, HK), 1)
    er = lax.broadcasted_iota(jnp.int32, (HV, HK), 0) // GLA_DV
    dc = lax.broadcasted_iota(jnp.int32, (HV, HK), 1) // GLA_DK
    bd = er == dc
    edge_row = 0 if reverse else C - 1
    mid_row = C // 2 if reverse else C // 2 - 1

    nbb = q_ref.shape[0]
    bbs = range(nbb)
    heads = range(GLA_HEADS)
    hms = [(lane // GLA_DK) == h for h in heads]
    las = [_log_sigmoid(_dot3(lr_ref[bb], up_ref[...]) + gb_ref[...]) * (1.0 / GLA_TAU) for bb in bbs]
    splits = [_split_bf16(la) for la in las]
    bcs = [_dot(tri, hi) + _dot(tri, lo) for hi, lo in splits]
    edges = [jnp.concatenate([jnp.broadcast_to(bc[i * C + edge_row:i * C + edge_row + 1], (C, HK))
                              for i in range(n_sub)], axis=0) for bc in bcs]
    mids = [jnp.concatenate([jnp.broadcast_to(bc[i * C + mid_row:i * C + mid_row + 1], (C, HK))
                             for i in range(n_sub)], axis=0) for bc in bcs]
    qs = [q_ref[bb].astype(F32) * (GLA_DK ** -0.5) for bb in bbs]
    ks = [k_ref[bb].astype(F32) for bb in bbs]
    vs = [v_ref[bb] for bb in bbs]
    qes = [(q * jnp.exp(bc - mid)).astype(BF16) for q, bc, mid in zip(qs, bcs, mids)]
    kes = [(k * jnp.exp(mid - bc)).astype(BF16) for k, bc, mid in zip(ks, bcs, mids)]
    atts = [[_dot_nt(jnp.where(hms[h], qes[bb], jnp.zeros_like(qes[bb])), kes[bb]) for h in heads] for bb in bbs]
    qbs = [(q * jnp.exp(bc)).astype(BF16) for q, bc in zip(qs, bcs)]
    kls = [(k * jnp.exp(edge - bc)).astype(BF16) for k, bc, edge in zip(ks, bcs, edges)]
    decs = [[jnp.exp(edge[i * C:i * C + 1]) for i in range(n_sub)] for edge in edges]
    atts = [[jnp.where(att_mask, atts[bb][h], 0.0).astype(BF16) for h in heads] for bb in bbs]
    us = [[_dot_tn(vs[bb][i * C:(i + 1) * C], kls[bb][i * C:(i + 1) * C]) for i in range(n_sub)] for bb in bbs]
    intras = [[_dot(atts[bb][h], vs[bb][:, h * GLA_DV:(h + 1) * GLA_DV]) for h in heads] for bb in bbs]

    sts = [st_ref[bb] for bb in bbs]
    inters = [[None] * n_sub for _ in bbs]
    order = range(n_sub - 1, -1, -1) if reverse else range(n_sub)
    for i in order:
        for bb in bbs:
            inters[bb][i] = _dot_nt(qbs[bb][i * C:(i + 1) * C], sts[bb].astype(BF16))
            sts[bb] = decs[bb][i] * sts[bb] + jnp.where(bd, us[bb][i], 0.0)
    for bb in bbs:
        st_ref[bb] = sts[bb]
        o = jnp.concatenate(intras[bb], axis=-1) + jnp.concatenate(inters[bb], axis=0)
        if combine:
            o = o + ob_ref[bb]
            outs = []
            for h in heads:
                oh = o[:, h * GLA_DV:(h + 1) * GLA_DV]
                outs.append(oh * lax.rsqrt(jnp.mean(oh * oh, axis=-1, keepdims=True) + RMS_EPS))
            o_ref[bb] = (jnp.concatenate(outs, axis=-1) * ng_ref[...]).astype(o_ref.dtype)
        else:
            o_ref[bb] = o


def gla_direction(P, LR, up_pad, gbias, B, L, *, reverse, o_other=None, norm_g=None):
    T = GLA_T
    nb = L // T
    nbb = GLA_BATCH_PER_STEP if B % GLA_BATCH_PER_STEP == 0 else 1
    HK, HV = GLA_HEADS * GLA_DK, GLA_HEADS * GLA_DV
    combine = o_other is not None
    P3 = P.reshape(B, L, P.shape[-1])
    LR3 = LR.reshape(B, L, LR_PAD)

    def tok(j):
        return nb - 1 - j if reverse else j

    seq = lambda w, col: pl.BlockSpec((nbb, T, w), lambda b, j: (b, tok(j), col))
    in_specs = [
        seq(HK, COL_GQ // HK), seq(HK, COL_GK // HK), seq(HV, COL_GV // HV), seq(LR_PAD, 0),
        pl.BlockSpec((LR_PAD, HK), lambda b, j: (0, 0)),
        pl.BlockSpec((1, HK), lambda b, j: (0, 0)),
    ]
    args = [P3, P3, P3, LR3, up_pad, gbias]
    if combine:
        in_specs += [seq(HV, 0), pl.BlockSpec((1, HV), lambda b, j: (0, 0))]
        args += [o_other.reshape(B, L, HV), norm_g]
    out = pl.pallas_call(
        functools.partial(_gla_body, reverse=reverse, combine=combine),
        grid=(B // nbb, nb),
        in_specs=in_specs,
        out_specs=seq(HV, 0),
        out_shape=jax.ShapeDtypeStruct((B, L, HV), BF16 if combine else F32),
        scratch_shapes=[pltpu.VMEM((nbb, HV, HK), F32)],
        compiler_params=_cp(("parallel", "arbitrary")),
        name="gla_fwd" if combine else "gla_bwd",
    )(*args)
    return out.reshape(B * L, HV)


def _hy_conv_body(v_ref, x0_ref, x1_ref, wv_ref, wx0_ref, wx1_ref, bv_ref, bx0_ref, bx1_ref,
                  u_ref, x0o_ref, *, L, R):
    n_chunks = L // R
    HALO = 16
    rid = lax.broadcasted_iota(jnp.int32, (R, LANES), 0)

    def conv(ref, w_ref, b_ref, i):
        base = pl.multiple_of(i * R, R)
        cur = ref[pl.ds(base, R), :].astype(F32)
        pstart = pl.multiple_of(jnp.maximum(base - HALO, 0), HALO)
        nstart = pl.multiple_of(jnp.minimum(base + R, L - HALO), HALO)
        prev_row = ref[pl.ds(pstart, HALO), :][HALO - 1:HALO].astype(F32) * jnp.where(i > 0, 1.0, 0.0)
        next_row = ref[pl.ds(nstart, HALO), :][0:1].astype(F32) * jnp.where(i < n_chunks - 1, 1.0, 0.0)
        up = jnp.where(rid == 0, prev_row, pltpu.roll(cur, 1, 0))
        dn = jnp.where(rid == R - 1, next_row, pltpu.roll(cur, R - 1, 0))
        w = w_ref[...]
        return b_ref[...] + up * w[0:1] + cur * w[1:2] + dn * w[2:3]

    def step(i, carry):
        base = pl.multiple_of(i * R, R)
        cv = conv(v_ref, wv_ref, bv_ref, i)
        cx1 = conv(x1_ref, wx1_ref, bx1_ref, i)
        u_ref[pl.ds(base, R), :] = (cx1 * cv).astype(BF16)
        x0o_ref[pl.ds(base, R), :] = conv(x0_ref, wx0_ref, bx0_ref, i).astype(BF16)
        return carry

    lax.fori_loop(0, n_chunks, step, 0)


def hy_short_conv(P, conv_w, conv_b, B, L, R=1024):
    nc = HY_W // LANES
    bv, bx0, bx1 = COL_HYV // LANES, COL_HYX0 // LANES, COL_HYX1 // LANES
    seq = lambda off: pl.BlockSpec((L, LANES), lambda b, c: (b, off + c))
    wsp = lambda seg: pl.BlockSpec((HY_CONV, LANES), lambda b, c: (0, seg * nc + c))
    bsp = lambda seg: pl.BlockSpec((1, LANES), lambda b, c: (0, seg * nc + c))
    out = pl.BlockSpec((L, LANES), lambda b, c: (b, c))
    return pl.pallas_call(
        functools.partial(_hy_conv_body, L=L, R=min(R, L)),
        grid=(B, nc),
        in_specs=[seq(bv), seq(bx0), seq(bx1), wsp(0), wsp(1), wsp(2), bsp(0), bsp(1), bsp(2)],
        out_specs=[out, out],
        out_shape=[jax.ShapeDtypeStruct((B * L, HY_W), BF16)] * 2,
        compiler_params=_cp(("parallel", "parallel")),
        name="hy_conv",
    )(P, P, P, conv_w, conv_w, conv_w, conv_b, conv_b, conv_b)


def _hy_filter_body(freq_ref, delta_ref, w1_ref, b1_ref, wm_ref, bm_ref, w3_ref, k_ref, e_ref, *, L, R):
    i = pl.program_id(0)
    t = i * R + lax.broadcasted_iota(jnp.int32, (R, 1), 0)
    back = t >= L
    pos = jnp.where(back, 2 * L - t, t).astype(F32)
    t01 = pos / float(L - 1)
    lane = lax.broadcasted_iota(jnp.int32, (R, LANES), 1)
    ang = ((2.0 * math.pi / L) * pos) * freq_ref[...]
    z = jnp.where(lane == 0, t01,
                  jnp.where(lane <= HY_EMB_BANDS, jnp.cos(ang),
                            jnp.where(lane < HY_EMB, -jnp.sin(ang), 0.0)))
    h = jnp.sin(HY_SIN_FREQ * (_dot3(z, w1_ref[...]) + b1_ref[...]))
    for j in range(HY_INNER_MLPS):
        h = jnp.sin(HY_SIN_FREQ * (_dot3(h, wm_ref[j]) + bm_ref[j]))
    hw = _dot3(h, w3_ref[...])
    window = jnp.exp(-t01 * delta_ref[...])
    taps = jnp.where(back, hw[:, HY_W:], hw[:, :HY_W]) * window
    taps = jnp.where(t == L, 0.0, taps)
    k_ref[...] = taps.astype(BF16)

    @pl.when(i == 0)
    def _():
        e_ref[...] = jnp.zeros_like(e_ref)

    e_ref[...] += jnp.sum(taps * taps, axis=0, keepdims=True)


def hy_filter(freq, delta, w1p, b1p, wmp, bmp, w3p, L, R=1024):
    R = min(R, L)
    full = lambda a: pl.BlockSpec(a.shape, lambda i, _n=a.ndim: (0,) * _n)
    args = (freq, delta, w1p, b1p, wmp, bmp, w3p)
    return pl.pallas_call(
        functools.partial(_hy_filter_body, L=L, R=R),
        grid=(2 * L // R,),
        in_specs=[full(a) for a in args],
        out_specs=[pl.BlockSpec((R, HY_W), lambda i: (i, 0)), pl.BlockSpec((1, HY_W), lambda i: (0, 0))],
        out_shape=[jax.ShapeDtypeStruct((2 * L, HY_W), BF16), jax.ShapeDtypeStruct((1, HY_W), F32)],
        compiler_params=_cp(("arbitrary",)),
        name="hy_filter",
    )(*args)


def _dft_slow_body(c_ref, x_ref, o_ref):
    two, half, t2n, w = x_ref.shape
    x = jnp.swapaxes(x_ref[...].reshape(two * half, t2n, w), 0, 1)
    c = c_ref[...]
    y = jnp.stack([_dot(c, x[t]).astype(BF16) for t in range(t2n)], axis=0)
    o_ref[...] = jnp.swapaxes(y, 0, 1).reshape(o_ref.shape)


def dft_slow(cmat, x5, t2n=DFT_T2_BLOCK):
    R, K = cmat.shape
    npair, _, half, n2, w = x5.shape
    assert K == 2 * half and n2 % t2n == 0
    return pl.pallas_call(
        _dft_slow_body,
        grid=(npair, n2 // t2n),
        in_specs=[
            pl.BlockSpec((R, K), lambda p, j: (0, 0)),
            pl.BlockSpec((None, 2, half, t2n, w), lambda p, j: (p, 0, 0, j, 0)),
        ],
        out_specs=pl.BlockSpec((None, 2, R // 2, t2n, w), lambda p, j: (p, 0, 0, j, 0)),
        out_shape=jax.ShapeDtypeStruct((npair, 2, R // 2, n2, w), BF16),
        compiler_params=_cp(("parallel", "parallel")),
        name="dft_slow",
    )(cmat, x5)


def _stack_complex(tab_ref, k):
    cr, ci = tab_ref[k, 0], tab_ref[k, 1]
    return jnp.concatenate([jnp.concatenate([cr, -ci], axis=1), jnp.concatenate([ci, cr], axis=1)], axis=0)


def _hy_kf_body(a_ref, cf_ref, e_ref, o_ref, *, F, n_fft):
    n2 = DFT_N2
    scale = lax.rsqrt(e_ref[...] + 1e-6) * (1.0 / n_fft)
    for k in range(F):
        a = jnp.concatenate([a_ref[0, k], a_ref[1, k]], axis=0)
        x = _dot(_stack_complex(cf_ref, k), a) * scale
        o_ref[k, 0] = x[:n2]
        o_ref[k, 1] = x[n2:]


def hy_filter_spectrum(a5, cf_tab, energy, n_fft, F=8):
    n1, n2 = n_fft // DFT_N2, DFT_N2
    F = min(F, n1)
    return pl.pallas_call(
        functools.partial(_hy_kf_body, F=F, n_fft=n_fft),
        grid=(n1 // F,),
        in_specs=[
            pl.BlockSpec((2, F, n2, HY_W), lambda f: (0, f, 0, 0)),
            pl.BlockSpec((F, 2, n2, n2), lambda f: (f, 0, 0, 0)),
            pl.BlockSpec((1, HY_W), lambda f: (0, 0)),
        ],
        out_specs=pl.BlockSpec((F, 2, n2, HY_W), lambda f: (f, 0, 0, 0)),
        out_shape=jax.ShapeDtypeStruct((n1, 2, n2, HY_W), F32),
        compiler_params=_cp(("parallel",)),
        name="hy_kf",
    )(a5, cf_tab, energy)


def _hy_mid_body(a_ref, kf_ref, cf_ref, ci_ref, o_ref, *, F):
    n2 = DFT_N2
    ks = range(F)
    xs = [_dot(_stack_complex(cf_ref, k), jnp.concatenate([a_ref[0, k], a_ref[1, k]], axis=0)) for k in ks]
    ys = []
    for k in ks:
        xr, xi = xs[k][:n2], xs[k][n2:]
        kr, ki = kf_ref[k, 0], kf_ref[k, 1]
        ys.append(jnp.concatenate([xr * kr - xi * ki, xr * ki + xi * kr], axis=0).astype(BF16))
    bvs = [_dot(_stack_complex(ci_ref, k), ys[k]) for k in ks]
    for k in ks:
        o_ref[0, k] = bvs[k][:n2].astype(BF16)
        o_ref[1, k] = bvs[k][n2:].astype(BF16)


def hy_mid(a5, kf, cf_tab, ci_tab, F=8):
    npair = a5.shape[0]
    n1, n2 = kf.shape[0], DFT_N2
    F = min(F, n1)
    return pl.pallas_call(
        functools.partial(_hy_mid_body, F=F),
        grid=(n1 // F, npair),
        in_specs=[
            pl.BlockSpec((None, 2, F, n2, HY_W), lambda f, p: (p, 0, f, 0, 0)),
            pl.BlockSpec((F, 2, n2, HY_W), lambda f, p: (f, 0, 0, 0)),
            pl.BlockSpec((F, 2, n2, n2), lambda f, p: (f, 0, 0, 0)),
            pl.BlockSpec((F, 2, n2, n2), lambda f, p: (f, 0, 0, 0)),
        ],
        out_specs=pl.BlockSpec((None, 2, F, n2, HY_W), lambda f, p: (p, 0, f, 0, 0)),
        out_shape=jax.ShapeDtypeStruct((npair, 2, n1, n2, HY_W), BF16),
        compiler_params=_cp(("parallel", "arbitrary")),
        name="hy_mid",
    )(a5, kf, cf_tab, ci_tab)


def _hy_out_body(c_ref, b_ref, u_ref, x0_ref, skip_ref, o_ref):
    two, n1, t2n, w = b_ref.shape
    b = jnp.swapaxes(b_ref[...].reshape(two * n1, t2n, w), 0, 1)
    c = c_ref[...]
    y = jnp.stack([_dot(c, b[t]) for t in range(t2n)], axis=0)
    y = jnp.swapaxes(y, 0, 1).reshape(o_ref.shape)
    conv = y + skip_ref[...] * u_ref[...].astype(F32)
    o_ref[...] = (x0_ref[...].astype(F32) * conv).astype(BF16)


def hy_out(cmat, bsp, u5, x05, skip, t2n=DFT_T2_BLOCK):
    npair, _, n1, n2, w = bsp.shape
    half = u5.shape[2]
    pair = pl.BlockSpec((None, 2, half, t2n, w), lambda p, j: (p, 0, 0, j, 0))
    return pl.pallas_call(
        _hy_out_body,
        grid=(npair, n2 // t2n),
        in_specs=[
            pl.BlockSpec((2 * half, 2 * n1), lambda p, j: (0, 0)),
            pl.BlockSpec((None, 2, n1, t2n, w), lambda p, j: (p, 0, 0, j, 0)),
            pair, pair,
            pl.BlockSpec((1, w), lambda p, j: (0, 0)),
        ],
        out_specs=pair,
        out_shape=jax.ShapeDtypeStruct((npair, 2, half, n2, w), BF16),
        compiler_params=_cp(("parallel", "parallel")),
        name="hy_out",
    )(cmat, bsp, u5, x05, skip)


@functools.lru_cache(maxsize=None)
def _dft_tables(L):
    n = 2 * L
    n2 = DFT_N2
    n1 = n // n2
    hl = n1 // 2
    f1 = np.arange(n1)
    t1 = np.arange(n1)
    th1 = 2.0 * np.pi * ((f1[:, None] * t1[None, :]) % n1) / n1
    fr, fi = np.cos(th1), -np.sin(th1)
    c1_data = np.block([[fr[:, :hl], -fi[:, :hl]], [fi[:, :hl], fr[:, :hl]]])
    c1_real = np.concatenate([fr, fi], axis=0)
    gr, gi = np.cos(th1.T)[:hl], np.sin(th1.T)[:hl]
    c1_inv = np.block([[gr, -gi], [gi, gr]])
    f2 = np.arange(n2)
    t2 = np.arange(n2)
    k = (t2[None, None, :] * (f1[:, None, None] + n1 * f2[None, :, None])) % n
    th = 2.0 * np.pi * k / n
    cf = np.stack([np.cos(th), -np.sin(th)], axis=1)
    tht = np.transpose(th, (0, 2, 1))
    ci = np.stack([np.cos(tht), np.sin(tht)], axis=1)
    return tuple(a.astype(np.float32).astype(BF16) for a in (c1_data, c1_real, c1_inv, cf, ci))


def hyena_branch(P, B, L, conv_w, conv_b, w1, b1, wm, bm, w3, skip):
    assert B % 2 == 0 and (2 * L) % DFT_N2 == 0 and L % DFT_N2 == 0
    n_fft = 2 * L
    n1, n2 = n_fft // DFT_N2, DFT_N2
    hl = n1 // 2
    NL = n2 * HY_W
    c1_data, c1_real, c1_inv, cf_tab, ci_tab = (jnp.asarray(a) for a in _dft_tables(L))

    hid = LANES
    freqs = np.linspace(1e-4, HY_EMB_BANDS - 1, HY_EMB_BANDS, dtype=np.float32)
    freq_lane = np.zeros((1, LANES), np.float32)
    freq_lane[0, 1:1 + HY_EMB_BANDS] = freqs
    freq_lane[0, 1 + HY_EMB_BANDS:HY_EMB] = freqs
    delta = np.abs(np.linspace(HY_MIN_DECAY, HY_MAX_DECAY, HY_W, dtype=np.float32)).reshape(1, HY_W)
    w1p = jnp.zeros((LANES, hid), F32).at[:HY_EMB, :HY_FILTER_HID].set(w1)
    b1p = jnp.zeros((1, hid), F32).at[0, :HY_FILTER_HID].set(b1)
    wmp = jnp.zeros((HY_INNER_MLPS, hid, hid), F32).at[:, :HY_FILTER_HID, :HY_FILTER_HID].set(wm)
    bmp = jnp.zeros((HY_INNER_MLPS, 1, hid), F32).at[:, 0, :HY_FILTER_HID].set(bm)
    w3p = jnp.zeros((hid, 2 * HY_W), F32).at[:HY_FILTER_HID].set(w3)
    kern, energy = hy_filter(jnp.asarray(freq_lane), jnp.asarray(delta), w1p, b1p, wmp, bmp, w3p, L)
    ak = dft_slow(c1_real, kern.reshape(1, 2, hl, n2, HY_W))[0]
    kf = hy_filter_spectrum(ak, cf_tab, energy, n_fft)

    u, x0c = hy_short_conv(P, conv_w, conv_b.reshape(1, -1), B, L)
    u5 = u.reshape(B // 2, 2, hl, n2, HY_W)
    x05 = x0c.reshape(B // 2, 2, hl, n2, HY_W)
    a = dft_slow(c1_data, u5)
    bsp = hy_mid(a, kf, cf_tab, ci_tab)
    return hy_out(c1_inv, bsp, u5, x05, skip.reshape(1, HY_W)).reshape(B * L, HY_W)


def _merge_body(ona_ref, ohy_ref, ogl_ref, omem_ref, z_ref, g_ref, wbr_ref, wout_ref, bout_ref,
                x_ref, lg_ref, lb_ref, h32_ref, *maybe_h16_ref, alpha):
    D = x_ref.shape[-1]
    branches = range(N_BRANCH)
    o_refs = (ona_ref, ohy_ref, ogl_ref, omem_ref)
    zs = [z_ref[:, j * BRANCH_W:(j + 1) * BRANCH_W] for j in branches]
    acts = [o_refs[j][...] * (zs[j] * (1.0 + jnp.tanh(zs[j]))) for j in branches]
    projs = [_dot(acts[j], wbr_ref[j]) for j in branches]
    gates = [1.0 + jnp.tanh(g_ref[:, j * D:(j + 1) * D].astype(F32)) for j in branches]
    merged = gates[0] * projs[0]
    for j in branches[1:]:
        merged = merged + gates[j] * projs[j]
    out = _dot(merged.astype(BF16), wout_ref[...]) + bout_ref[...]
    h = _layer_norm_rows(alpha * x_ref[...] + out, lg_ref[...], lb_ref[...])
    h32_ref[...] = h
    for h16_ref in maybe_h16_ref:
        h16_ref[...] = h.astype(BF16)


def merge_layer(o_na, o_hy, o_gla, o_mem, P, wbr, wout, bout, x32, lg, lb, alpha, want_bf16, tm=512):
    M, D = x32.shape
    row = lambda w, blk=0: pl.BlockSpec((tm, w), lambda i, _b=blk: (i, _b))
    vec = pl.BlockSpec((1, D), lambda i: (0, 0))
    n_out = 2 if want_bf16 else 1
    outs = pl.pallas_call(
        functools.partial(_merge_body, alpha=alpha),
        grid=(M // tm,),
        in_specs=[
            row(BRANCH_W), row(BRANCH_W), row(BRANCH_W), row(BRANCH_W),
            row(N_BRANCH * BRANCH_W, COL_Z // (N_BRANCH * BRANCH_W)),
            row(N_BRANCH * D, COL_G // (N_BRANCH * D)),
            pl.BlockSpec((N_BRANCH, BRANCH_W, D), lambda i: (0, 0, 0), pipeline_mode=pl.Buffered(1)),
            pl.BlockSpec((D, D), lambda i: (0, 0), pipeline_mode=pl.Buffered(1)),
            vec, row(D), vec, vec,
        ],
        out_specs=[row(D), row(D)][:n_out],
        out_shape=[jax.ShapeDtypeStruct((M, D), F32), jax.ShapeDtypeStruct((M, D), BF16)][:n_out],
        compiler_params=_cp(("parallel",)),
        name="merge",
    )(o_na, o_hy, o_gla, o_mem, P, P, wbr, wout, bout, x32, lg, lb)
    return (outs[0], outs[1]) if want_bf16 else (outs[0], None)


def _permute_in_proj(w_in, b_in):
    sizes = (512, 512, 512, 1536, 256, 256, 512, GLA_RANK, GLA_RANK, 512, 2048, 4096)
    offs = np.concatenate([[0], np.cumsum(sizes)])
    seg = lambda a, i: a[..., offs[i]:offs[i + 1]]
    order = (11, 10, 0, 1, 2, 3, 4, 5, 6, 9)
    col_scale = {0: NA_QSCALE, 10: 0.5, 11: 0.5}
    scaled = lambda a, i: seg(a, i) * col_scale[i] if i in col_scale else seg(a, i)
    wm = jnp.concatenate([scaled(w_in, i) for i in order], axis=-1).astype(BF16)
    bm = jnp.concatenate([scaled(b_in, i) for i in order], axis=-1).reshape(1, -1)
    pad = LR_PAD - 2 * GLA_RANK
    wlr = jnp.pad(jnp.concatenate([seg(w_in, 7), seg(w_in, 8)], axis=-1), ((0, 0), (0, pad))).astype(BF16)
    blr = jnp.pad(jnp.concatenate([seg(b_in, 7), seg(b_in, 8)], axis=-1), ((0, pad),)).reshape(1, -1)
    return wm, bm, wlr, blr


def _hybrid_layer(h32, h16, mem16, B, L, depth, want_bf16, w_in, b_in, na_rpb, hy_conv_w, hy_conv_b, hy_w1, hy_b1,
                  hy_wm, hy_bm, hy_w3, hy_skip, gla_up_f, gla_bias_f, gla_up_b, gla_bias_b, gla_norm_g,
                  mem_w_kv, w_br, w_out, b_out, ln_g, ln_b):
    D = h32.shape[-1]
    rows = L // GRID_W
    assert rows % NA_G == 0 and rows >= NA_KR and L % GLA_T == 0
    wm, bm, wlr, blr = _permute_in_proj(w_in, b_in)
    assert wm.shape[1] == N_MAIN
    P, LR = in_proj(h16, wm, bm, wlr, blr)

    ng = rows // NA_G
    geo = [_na_group_geometry(g, rows) for g in range(ng)]
    assert all(geo[g][2] == geo[1][2] for g in range(1, ng - 1))
    rpb_flat = na_rpb.reshape(-1)
    bias = jnp.stack([na_bias_tile(rpb_flat, geo[g][2]) for g in (0, 1, ng - 1)], axis=0)
    o_na = na_attention(P, bias, B, L)

    o_hy = hyena_branch(P, B, L, hy_conv_w, hy_conv_b, hy_w1, hy_b1, hy_wm, hy_bm, hy_w3, hy_skip)

    HK = GLA_HEADS * GLA_DK
    up_f = jnp.zeros((LR_PAD, HK), F32).at[:GLA_RANK].set(gla_up_f)
    up_b = jnp.zeros((LR_PAD, HK), F32).at[GLA_RANK:2 * GLA_RANK].set(gla_up_b)
    o_b = gla_direction(P, LR, up_b, gla_bias_b.reshape(1, HK), B, L, reverse=True)
    o_gla = gla_direction(P, LR, up_f, gla_bias_f.reshape(1, HK), B, L, reverse=False,
                          o_other=o_b, norm_g=gla_norm_g.reshape(1, -1))

    n_mem = mem16.shape[0] // B
    kv = matmul_bf16(mem16, mem_w_kv.astype(BF16))
    o_mem = mem_attention(P, kv, B, L, n_mem)

    alpha = (2 * depth) ** 0.25
    return merge_layer(o_na, o_hy, o_gla, o_mem, P, (0.5 * w_br).astype(BF16), w_out.astype(BF16),
                       b_out.reshape(1, D), h32, ln_g.reshape(1, D), ln_b.reshape(1, D), alpha, want_bf16)


def kernel(x, mem, ln_in_g, ln_in_b, w_in, b_in, na_rpb, hy_conv_w, hy_conv_b, hy_w1, hy_b1, hy_wm, hy_bm,
           hy_w3, hy_skip, gla_up_f, gla_bias_f, gla_up_b, gla_bias_b, gla_norm_g, mem_w_kv, w_br, w_out,
           b_out, ln_g, ln_b):
    B, L, D = x.shape
    depth = w_in.shape[0]
    h32, h16 = ln_in(x.reshape(B * L, D), ln_in_g, ln_in_b)
    mem16 = mem.reshape(-1, D).astype(BF16)
    for i in range(depth):
        h32, h16 = _hybrid_layer(
            h32, h16, mem16, B, L, depth, i + 1 < depth, w_in[i], b_in[i], na_rpb[i], hy_conv_w[i], hy_conv_b[i], hy_w1[i],
            hy_b1[i], hy_wm[i], hy_bm[i], hy_w3[i], hy_skip[i], gla_up_f[i], gla_bias_f[i], gla_up_b[i],
            gla_bias_b[i], gla_norm_g[i], mem_w_kv[i], w_br[i], w_out[i], b_out[i], ln_g[i], ln_b[i])
    return h32.reshape(B, L, D)
```

```python
import functools
import math

import numpy as np
import jax
import jax.numpy as jnp
from jax import lax
from jax.experimental import pallas as pl
from jax.experimental.pallas import tpu as pltpu

F32 = jnp.float32
BF16 = jnp.bfloat16

GRID_W = 64
NA_HEADS, NA_HEAD_DIM, NA_WIN_R, NA_WIN_C = 8, 64, 8, 16
HY_W, HY_CONV, HY_EMB_BANDS, HY_FILTER_HID, HY_INNER_MLPS = 512, 3, 16, 64, 2
HY_EMB = 1 + 2 * HY_EMB_BANDS
HY_SIN_FREQ = 1.0
HY_DECAY_TARGET, HY_FAST_DECAY_PCT, HY_SLOW_DECAY_PCT = 1e-2, 0.3, 1.5
HY_MAX_DECAY = math.log(HY_DECAY_TARGET) / HY_FAST_DECAY_PCT
HY_MIN_DECAY = math.log(HY_DECAY_TARGET) / HY_SLOW_DECAY_PCT
GLA_HEADS, GLA_DK, GLA_DV, GLA_RANK, GLA_TAU, GLA_CHUNK = 4, 64, 128, 16, 16.0, 64
MEM_HEADS, MEM_HEAD_DIM = 4, 128
N_BRANCH, BRANCH_W = 4, 512
LN_EPS, RMS_EPS = 1e-5, 1e-6

LANES = 128
VMEM_LIMIT = 52 * 1024 * 1024
NEG_BIG = -1e30

COL_G = 0
COL_Z = 4096
COL_NAQ, COL_NAK, COL_NAV = 6144, 6656, 7168
COL_HYV, COL_HYX0, COL_HYX1 = 7680, 8192, 8704
COL_GQ, COL_GK, COL_GV = 9216, 9472, 9728
COL_MQ = 10240
N_MAIN = 10752
LR_PAD = 128

LOG2E = math.log2(math.e)
NA_QSCALE = NA_HEAD_DIM ** -0.5 * LOG2E
NA_G = 4
NA_KR = NA_G + NA_WIN_R - 1
NA_GROUPS_PER_STEP = 8
GLA_T = 256
GLA_BATCH_PER_STEP = 8
DFT_N2 = 128
DFT_T2_BLOCK = 16


def _cp(sem):
    return pltpu.CompilerParams(dimension_semantics=sem, vmem_limit_bytes=VMEM_LIMIT)


def _split_bf16(a):
    hi = a.astype(BF16)
    lo = (a - hi.astype(F32)).astype(BF16)
    return hi, lo


def _dot(a, b):
    return jnp.dot(a, b, preferred_element_type=F32)


def _dot3(a, b):
    ah, al = _split_bf16(a)
    bh, bl = _split_bf16(b)
    return _dot(ah, bh) + _dot(al, bh) + _dot(ah, bl)


def _dot_nt(a, b):
    return lax.dot_general(a, b, (((1,), (1,)), ((), ())), preferred_element_type=F32)


def _dot_tn(a, b):
    return lax.dot_general(a, b, (((0,), (0,)), ((), ())), preferred_element_type=F32)


def _layer_norm_rows(y, g, b):
    mu = jnp.mean(y, axis=-1, keepdims=True)
    d = y - mu
    var = jnp.mean(d * d, axis=-1, keepdims=True)
    return d * lax.rsqrt(var + LN_EPS) * g + b


def _ln_in_body(x_ref, g_ref, b_ref, h32_ref, h16_ref):
    h = _layer_norm_rows(x_ref[...], g_ref[...], b_ref[...])
    h32_ref[...] = h
    h16_ref[...] = h.astype(BF16)


def ln_in(x2, g, b, tm=1024):
    M, D = x2.shape
    row = pl.BlockSpec((tm, D), lambda i: (i, 0))
    vec = pl.BlockSpec((1, D), lambda i: (0, 0))
    return pl.pallas_call(
        _ln_in_body,
        grid=(M // tm,),
        in_specs=[row, vec, vec],
        out_specs=[row, row],
        out_shape=[jax.ShapeDtypeStruct((M, D), F32), jax.ShapeDtypeStruct((M, D), BF16)],
        compiler_params=_cp(("parallel",)),
        name="ln_in",
    )(x2, g.reshape(1, D), b.reshape(1, D))


def _in_proj_body(x_ref, w_ref, b_ref, wlr_ref, blr_ref, p_ref, lr_ref):
    x = x_ref[...]
    p_ref[...] = (_dot(x, w_ref[...]) + b_ref[...]).astype(BF16)

    @pl.when(pl.program_id(1) == 0)
    def _():
        lr_ref[...] = _dot(x, wlr_ref[...]) + blr_ref[...]


def in_proj(h16, wm, bm, wlr, blr, tm=2048, tn=1536):
    M, D = h16.shape
    NM = wm.shape[1]
    return pl.pallas_call(
        _in_proj_body,
        grid=(M // tm, NM // tn),
        in_specs=[
            pl.BlockSpec((tm, D), lambda i, j: (i, 0)),
            pl.BlockSpec((D, tn), lambda i, j: (0, j)),
            pl.BlockSpec((1, tn), lambda i, j: (0, j)),
            pl.BlockSpec((D, LR_PAD), lambda i, j: (0, 0)),
            pl.BlockSpec((1, LR_PAD), lambda i, j: (0, 0)),
        ],
        out_specs=[
            pl.BlockSpec((tm, tn), lambda i, j: (i, j)),
            pl.BlockSpec((tm, LR_PAD), lambda i, j: (i, 0)),
        ],
        out_shape=[jax.ShapeDtypeStruct((M, NM), BF16), jax.ShapeDtypeStruct((M, LR_PAD), F32)],
        compiler_params=_cp(("parallel", "arbitrary")),
        name="in_proj",
    )(h16, wm, bm, wlr, blr)


def _matmul_body(x_ref, w_ref, o_ref):
    o_ref[...] = _dot(x_ref[...], w_ref[...]).astype(o_ref.dtype)


def matmul_bf16(x, w, tm=512):
    M, K = x.shape
    N = w.shape[1]
    return pl.pallas_call(
        _matmul_body,
        grid=(M // tm,),
        in_specs=[pl.BlockSpec((tm, K), lambda i: (i, 0)), pl.BlockSpec((K, N), lambda i: (0, 0))],
        out_specs=pl.BlockSpec((tm, N), lambda i: (i, 0)),
        out_shape=jax.ShapeDtypeStruct((M, N), BF16),
        compiler_params=_cp(("parallel",)),
        name="mem_kv",
    )(x, w)


def _na_row_start(r, rows):
    return min(max(r - NA_WIN_R // 2, 0), rows - NA_WIN_R)


def _na_group_geometry(g, rows):
    r0 = g * NA_G
    kb = min(max(r0 - NA_WIN_R // 2, 0), rows - NA_KR)
    pat = []
    for i in range(NA_G):
        rs = _na_row_start(r0 + i, rows)
        pat.append(tuple((kb + j - (r0 + i) + NA_WIN_R - 1) if rs <= kb + j < rs + NA_WIN_R else None
                         for j in range(NA_KR)))
    return r0, kb, tuple(pat)


def _na_bias_body(rpb_ref, o_ref, *, pattern):
    h = pl.program_id(0)
    n_dc = 2 * NA_WIN_C - 1
    base = h * ((2 * NA_WIN_R - 1) * n_dc)
    cc = lax.broadcasted_iota(jnp.int32, (GRID_W, LANES), 0)
    lane = lax.broadcasted_iota(jnp.int32, (GRID_W, LANES), 1)
    c = lane % GRID_W
    left = lane < GRID_W
    cs = jnp.clip(c - NA_WIN_C // 2, 0, GRID_W - NA_WIN_C)
    col_ok = (cc >= cs) & (cc < cs + NA_WIN_C)
    dcol = cc - c + NA_WIN_C - 1
    neg = jnp.full((GRID_W, LANES), NEG_BIG, F32)
    cache = {}

    def pair_tile(a0, a1):
        key = (a0, a1)
        if key in cache:
            return cache[key]
        t = neg
        for d in range(n_dc):
            v0 = rpb_ref[base + a0 * n_dc + d] * LOG2E if a0 is not None else NEG_BIG
            v1 = rpb_ref[base + a1 * n_dc + d] * LOG2E if a1 is not None else NEG_BIG
            val = jnp.where(left, v0, v1)
            t = jnp.where(col_ok & (dcol == d), val, t)
        cache[key] = t
        return t

    assert NA_G % 2 == 0
    for j in range(NA_KR):
        for ip in range(NA_G // 2):
            a0, a1 = pattern[2 * ip][j], pattern[2 * ip + 1][j]
            t = neg if (a0 is None and a1 is None) else pair_tile(a0, a1)
            o_ref[0, j * GRID_W:(j + 1) * GRID_W, ip * LANES:(ip + 1) * LANES] = t


def na_bias_tile(rpb_flat, pattern):
    tq, tk = NA_G * GRID_W, NA_KR * GRID_W
    return pl.pallas_call(
        functools.partial(_na_bias_body, pattern=pattern),
        grid=(NA_HEADS,),
        in_specs=[pl.BlockSpec(memory_space=pltpu.SMEM)],
        out_specs=pl.BlockSpec((1, tk, tq), lambda h: (h, 0, 0)),
        out_shape=jax.ShapeDtypeStruct((NA_HEADS, tk, tq), F32),
        compiler_params=_cp(("parallel",)),
        name="na_bias",
    )(rpb_flat)


def _na_attn_body(q_ref, k_ref, v_ref, *rest, rows):
    (bias_first, bias_mid, bias_last), o_ref = rest[:-1], rest[-1]
    tq, tk = NA_G * GRID_W, NA_KR * GRID_W
    n_grp = q_ref.shape[0] // tq
    bias_refs = [bias_first] + [bias_mid] * (n_grp - 2) + [bias_last]
    g0 = pl.program_id(2) * n_grp
    kws, vws, qms, biases = [], [], [], []
    for a in range(n_grp):
        kb = jnp.clip((g0 + a) * NA_G - NA_WIN_R // 2, 0, rows - NA_KR)
        start = pl.multiple_of(kb * GRID_W, GRID_W)
        kw = k_ref[pl.ds(start, tk), :]
        vw = v_ref[pl.ds(start, tk), :]
        q = q_ref[a * tq:(a + 1) * tq, :]
        first = lax.broadcasted_iota(jnp.int32, q.shape, 1) < NA_HEAD_DIM
        zero = jnp.zeros_like(q)
        for hh in range(2):
            kws.append(kw)
            vws.append(vw)
            qms.append(jnp.where(first if hh == 0 else jnp.logical_not(first), q, zero))
            biases.append(bias_refs[a][0, hh])
    sts = [_dot_nt(kw, qm) + b for kw, qm, b in zip(kws, qms, biases)]
    ms = [jnp.max(st, axis=0, keepdims=True) for st in sts]
    ps = [jnp.exp2(st - m) for st, m in zip(sts, ms)]
    ls = [jnp.sum(p, axis=0, keepdims=True) for p in ps]
    outs = [_dot_tn(vw, p.astype(BF16)) / l for vw, p, l in zip(vws, ps, ls)]
    top = lax.broadcasted_iota(jnp.int32, outs[0].shape, 0) < NA_HEAD_DIM
    for a in range(n_grp):
        o_ref[a * tq:(a + 1) * tq, :] = jnp.where(top, outs[2 * a], outs[2 * a + 1]).T.astype(BF16)


def na_attention(P, bias, B, L):
    rows = L // GRID_W
    ng = rows // NA_G
    n_grp = max(d for d in range(2, NA_GROUPS_PER_STEP + 1) if ng % d == 0)
    ns = ng // n_grp
    tq, tk = NA_G * GRID_W, NA_KR * GRID_W
    hp_n = NA_HEADS // 2
    qb, kb_, vb = COL_NAQ // LANES, COL_NAK // LANES, COL_NAV // LANES
    bias_specs = [
        pl.BlockSpec((1, 2, tk, tq), lambda b, hp, s: (jnp.where(s == 0, 0, 1), hp, 0, 0)),
        pl.BlockSpec((1, 2, tk, tq), lambda b, hp, s: (1, hp, 0, 0)),
        pl.BlockSpec((1, 2, tk, tq), lambda b, hp, s: (jnp.where(s == ns - 1, 2, 1), hp, 0, 0)),
    ]

    return pl.pallas_call(
        functools.partial(_na_attn_body, rows=rows),
        grid=(B, hp_n, ns),
        in_specs=[
            pl.BlockSpec((n_grp * tq, LANES), lambda b, hp, s: (b * ns + s, qb + hp)),
            pl.BlockSpec((L, LANES), lambda b, hp, s: (b, kb_ + hp)),
            pl.BlockSpec((L, LANES), lambda b, hp, s: (b, vb + hp)),
        ] + bias_specs,
        out_specs=pl.BlockSpec((n_grp * tq, LANES), lambda b, hp, s: (b * ns + s, hp)),
        out_shape=jax.ShapeDtypeStruct((B * L, NA_HEADS * NA_HEAD_DIM), BF16),
        compiler_params=_cp(("parallel", "parallel", "arbitrary")),
        name="na_attn",
    )(P, P, P, bias, bias, bias)


def _mem_attn_body(q_ref, kv_ref, o_ref):
    hd = MEM_HEAD_DIM
    outs = []
    for h in range(MEM_HEADS):
        q = q_ref[:, h * hd:(h + 1) * hd]
        k = kv_ref[:, h * hd:(h + 1) * hd]
        v = kv_ref[:, (MEM_HEADS + h) * hd:(MEM_HEADS + h + 1) * hd]
        s = _dot_nt(q, k) * (hd ** -0.5)
        m = jnp.max(s, axis=-1, keepdims=True)
        p = jnp.exp(s - m)
        l = jnp.sum(p, axis=-1, keepdims=True)
        outs.append(_dot(p.astype(BF16), v) / l)
    o_ref[...] = jnp.concatenate(outs, axis=-1).astype(BF16)


def mem_attention(P, kv, B, L, n_mem, tq=1024):
    W = MEM_HEADS * MEM_HEAD_DIM
    nt = L // tq
    return pl.pallas_call(
        _mem_attn_body,
        grid=(B, nt),
        in_specs=[
            pl.BlockSpec((tq, W), lambda b, t: (b * nt + t, COL_MQ // W)),
            pl.BlockSpec((n_mem, 2 * W), lambda b, t: (b, 0)),
        ],
        out_specs=pl.BlockSpec((tq, W), lambda b, t: (b * nt + t, 0)),
        out_shape=jax.ShapeDtypeStruct((B * L, W), BF16),
        compiler_params=_cp(("parallel", "parallel")),
        name="mem_attn",
    )(P, kv)


def _log_sigmoid(x):
    return jnp.minimum(x, 0.0) - jnp.log(1.0 + jnp.exp(-jnp.abs(x)))


def _gla_body(*refs, reverse, combine):
    if combine:
        q_ref, k_ref, v_ref, lr_ref, up_ref, gb_ref, ob_ref, ng_ref, o_ref, st_ref = refs
    else:
        q_ref, k_ref, v_ref, lr_ref, up_ref, gb_ref, o_ref, st_ref = refs
    T, C = GLA_T, GLA_CHUNK
    n_sub = T // C
    HK = GLA_HEADS * GLA_DK
    HV = GLA_HEADS * GLA_DV

    @pl.when(pl.program_id(1) == 0)
    def _():
        st_ref[...] = jnp.zeros_like(st_ref)

    r = lax.broadcasted_iota(jnp.int32, (T, T), 0)
    s = lax.broadcasted_iota(jnp.int32, (T, T), 1)
    same = (r // C) == (s // C)
    if reverse:
        csum_mask = same & (s >= r)
        att_mask = same & (s > r)
    else:
        csum_mask = same & (s <= r)
        att_mask = same & (s <= r)
    tri = jnp.where(csum_mask, 1.0, 0.0).astype(BF16)
    lane = lax.broadcasted_iota(jnp.int32, (T, HK), 1)
    er = lax.broadcasted_iota(jnp.int32, (HV, HK), 0) // GLA_DV
    dc = lax.broadcasted_iota(jnp.int32, (HV, HK), 1) // GLA_DK
    bd = er == dc
    edge_row = 0 if reverse else C - 1
    mid_row = C // 2 if reverse else C // 2 - 1

    nbb = q_ref.shape[0]
    bbs = range(nbb)
    heads = range(GLA_HEADS)
    hms = [(lane // GLA_DK) == h for h in heads]
    las = [_log_sigmoid(_dot3(lr_ref[bb], up_ref[...]) + gb_ref[...]) * (1.0 / GLA_TAU) for bb in bbs]
    splits = [_split_bf16(la) for la in las]
    bcs = [_dot(tri, hi) + _dot(tri, lo) for hi, lo in splits]
    edges = [jnp.concatenate([jnp.broadcast_to(bc[i * C + edge_row:i * C + edge_row + 1], (C, HK))
                              for i in range(n_sub)], axis=0) for bc in bcs]
    mids = [jnp.concatenate([jnp.broadcast_to(bc[i * C + mid_row:i * C + mid_row + 1], (C, HK))
                             for i in range(n_sub)], axis=0) for bc in bcs]
    qs = [q_ref[bb].astype(F32) * (GLA_DK ** -0.5) for bb in bbs]
    ks = [k_ref[bb].astype(F32) for bb in bbs]
    vs = [v_ref[bb] for bb in bbs]
    qes = [(q * jnp.exp(bc - mid)).astype(BF16) for q, bc, mid in zip(qs, bcs, mids)]
    kes = [(k * jnp.exp(mid - bc)).astype(BF16) for k, bc, mid in zip(ks, bcs, mids)]
    atts = [[_dot_nt(jnp.where(hms[h], qes[bb], jnp.zeros_like(qes[bb])), kes[bb]) for h in heads] for bb in bbs]
    qbs = [(q * jnp.exp(bc)).astype(BF16) for q, bc in zip(qs, bcs)]
    kls = [(k * jnp.exp(edge - bc)).astype(BF16) for k, bc, edge in zip(ks, bcs, edges)]
    decs = [[jnp.exp(edge[i * C:i * C + 1]) for i in range(n_sub)] for edge in edges]
    atts = [[jnp.where(att_mask, atts[bb][h], 0.0).astype(BF16) for h in heads] for bb in bbs]
    us = [[_dot_tn(vs[bb][i * C:(i + 1) * C], kls[bb][i * C:(i + 1) * C]) for i in range(n_sub)] for bb in bbs]
    intras = [[_dot(atts[bb][h], vs[bb][:, h * GLA_DV:(h + 1) * GLA_DV]) for h in heads] for bb in bbs]

    sts = [st_ref[bb] for bb in bbs]
    inters = [[None] * n_sub for _ in bbs]
    order = range(n_sub - 1, -1, -1) if reverse else range(n_sub)
    for i in order:
        for bb in bbs:
            inters[bb][i] = _dot_nt(qbs[bb][i * C:(i + 1) * C], sts[bb].astype(BF16))
            sts[bb] = decs[bb][i] * sts[bb] + jnp.where(bd, us[bb][i], 0.0)
    for bb in bbs:
        st_ref[bb] = sts[bb]
        o = jnp.concatenate(intras[bb], axis=-1) + jnp.concatenate(inters[bb], axis=0)
        if combine:
            o = o + ob_ref[bb].astype(F32)
            outs = []
            for h in heads:
                oh = o[:, h * GLA_DV:(h + 1) * GLA_DV]
                outs.append(oh * lax.rsqrt(jnp.mean(oh * oh, axis=-1, keepdims=True) + RMS_EPS))
            o_ref[bb] = (jnp.concatenate(outs, axis=-1) * ng_ref[...]).astype(o_ref.dtype)
        else:
            o_ref[bb] = o.astype(o_ref.dtype)


def gla_direction(P, LR, up_pad, gbias, B, L, *, reverse, o_other=None, norm_g=None):
    T = GLA_T
    nb = L // T
    nbb = GLA_BATCH_PER_STEP if B % GLA_BATCH_PER_STEP == 0 else 1
    HK, HV = GLA_HEADS * GLA_DK, GLA_HEADS * GLA_DV
    combine = o_other is not None
    P3 = P.reshape(B, L, P.shape[-1])
    LR3 = LR.reshape(B, L, LR_PAD)

    def tok(j):
        return nb - 1 - j if reverse else j

    seq = lambda w, col: pl.BlockSpec((nbb, T, w), lambda b, j: (b, tok(j), col))
    in_specs = [
        seq(HK, COL_GQ // HK), seq(HK, COL_GK // HK), seq(HV, COL_GV // HV), seq(LR_PAD, 0),
        pl.BlockSpec((LR_PAD, HK), lambda b, j: (0, 0)),
        pl.BlockSpec((1, HK), lambda b, j: (0, 0)),
    ]
    args = [P3, P3, P3, LR3, up_pad, gbias]
    if combine:
        in_specs += [seq(HV, 0), pl.BlockSpec((1, HV), lambda b, j: (0, 0))]
        args += [o_other.reshape(B, L, HV), norm_g]
    out = pl.pallas_call(
        functools.partial(_gla_body, reverse=reverse, combine=combine),
        grid=(B // nbb, nb),
        in_specs=in_specs,
        out_specs=seq(HV, 0),
        out_shape=jax.ShapeDtypeStruct((B, L, HV), BF16),
        scratch_shapes=[pltpu.VMEM((nbb, HV, HK), F32)],
        compiler_params=_cp(("parallel", "arbitrary")),
        name="gla_fwd" if combine else "gla_bwd",
    )(*args)
    return out.reshape(B * L, HV)


def _hy_conv_body(v_ref, x0_ref, x1_ref, wv_ref, wx0_ref, wx1_ref, bv_ref, bx0_ref, bx1_ref,
                  u_ref, x0o_ref, *, L, R):
    n_chunks = L // R
    HALO = 16
    rid = lax.broadcasted_iota(jnp.int32, (R, LANES), 0)

    def conv(ref, w_ref, b_ref, i):
        base = pl.multiple_of(i * R, R)
        cur = ref[pl.ds(base, R), :].astype(F32)
        pstart = pl.multiple_of(jnp.maximum(base - HALO, 0), HALO)
        nstart = pl.multiple_of(jnp.minimum(base + R, L - HALO), HALO)
        prev_row = ref[pl.ds(pstart, HALO), :][HALO - 1:HALO].astype(F32) * jnp.where(i > 0, 1.0, 0.0)
        next_row = ref[pl.ds(nstart, HALO), :][0:1].astype(F32) * jnp.where(i < n_chunks - 1, 1.0, 0.0)
        up = jnp.where(rid == 0, prev_row, pltpu.roll(cur, 1, 0))
        dn = jnp.where(rid == R - 1, next_row, pltpu.roll(cur, R - 1, 0))
        w = w_ref[...]
        return b_ref[...] + up * w[0:1] + cur * w[1:2] + dn * w[2:3]

    def step(i, carry):
        base = pl.multiple_of(i * R, R)
        cv = conv(v_ref, wv_ref, bv_ref, i)
        cx1 = conv(x1_ref, wx1_ref, bx1_ref, i)
        u_ref[pl.ds(base, R), :] = (cx1 * cv).astype(BF16)
        x0o_ref[pl.ds(base, R), :] = conv(x0_ref, wx0_ref, bx0_ref, i).astype(BF16)
        return carry

    lax.fori_loop(0, n_chunks, step, 0)


def hy_short_conv(P, conv_w, conv_b, B, L, R=1024):
    nc = HY_W // LANES
    bv, bx0, bx1 = COL_HYV // LANES, COL_HYX0 // LANES, COL_HYX1 // LANES
    seq = lambda off: pl.BlockSpec((L, LANES), lambda b, c: (b, off + c))
    wsp = lambda seg: pl.BlockSpec((HY_CONV, LANES), lambda b, c: (0, seg * nc + c))
    bsp = lambda seg: pl.BlockSpec((1, LANES), lambda b, c: (0, seg * nc + c))
    out = pl.BlockSpec((L, LANES), lambda b, c: (b, c))
    return pl.pallas_call(
        functools.partial(_hy_conv_body, L=L, R=min(R, L)),
        grid=(B, nc),
        in_specs=[seq(bv), seq(bx0), seq(bx1), wsp(0), wsp(1), wsp(2), bsp(0), bsp(1), bsp(2)],
        out_specs=[out, out],
        out_shape=[jax.ShapeDtypeStruct((B * L, HY_W), BF16)] * 2,
        compiler_params=_cp(("parallel", "parallel")),
        name="hy_conv",
    )(P, P, P, conv_w, conv_w, conv_w, conv_b, conv_b, conv_b)


def _hy_filter_body(freq_ref, delta_ref, w1_ref, b1_ref, wm_ref, bm_ref, w3_ref, k_ref, e_ref, *, L, R):
    i = pl.program_id(0)
    t = i * R + lax.broadcasted_iota(jnp.int32, (R, 1), 0)
    back = t >= L
    pos = jnp.where(back, 2 * L - t, t).astype(F32)
    t01 = pos / float(L - 1)
    lane = lax.broadcasted_iota(jnp.int32, (R, LANES), 1)
    ang = ((2.0 * math.pi / L) * pos) * freq_ref[...]
    z = jnp.where(lane == 0, t01,
                  jnp.where(lane <= HY_EMB_BANDS, jnp.cos(ang),
                            jnp.where(lane < HY_EMB, -jnp.sin(ang), 0.0)))
    h = jnp.sin(HY_SIN_FREQ * (_dot3(z, w1_ref[...]) + b1_ref[...]))
    for j in range(HY_INNER_MLPS):
        h = jnp.sin(HY_SIN_FREQ * (_dot3(h, wm_ref[j]) + bm_ref[j]))
    hw = _dot3(h, w3_ref[...])
    window = jnp.exp(-t01 * delta_ref[...])
    taps = jnp.where(back, hw[:, HY_W:], hw[:, :HY_W]) * window
    taps = jnp.where(t == L, 0.0, taps)
    k_ref[...] = taps.astype(BF16)

    @pl.when(i == 0)
    def _():
        e_ref[...] = jnp.zeros_like(e_ref)

    e_ref[...] += jnp.sum(taps * taps, axis=0, keepdims=True)


def hy_filter(freq, delta, w1p, b1p, wmp, bmp, w3p, L, R=1024):
    R = min(R, L)
    full = lambda a: pl.BlockSpec(a.shape, lambda i, _n=a.ndim: (0,) * _n)
    args = (freq, delta, w1p, b1p, wmp, bmp, w3p)
    return pl.pallas_call(
        functools.partial(_hy_filter_body, L=L, R=R),
        grid=(2 * L // R,),
        in_specs=[full(a) for a in args],
        out_specs=[pl.BlockSpec((R, HY_W), lambda i: (i, 0)), pl.BlockSpec((1, HY_W), lambda i: (0, 0))],
        out_shape=[jax.ShapeDtypeStruct((2 * L, HY_W), BF16), jax.ShapeDtypeStruct((1, HY_W), F32)],
        compiler_params=_cp(("arbitrary",)),
        name="hy_filter",
    )(*args)


def _dft_slow_body(c_ref, x_ref, o_ref):
    two, half, t2n, w = x_ref.shape
    x = jnp.swapaxes(x_ref[...].reshape(two * half, t2n, w), 0, 1)
    c = c_ref[...]
    y = jnp.stack([_dot(c, x[t]).astype(BF16) for t in range(t2n)], axis=0)
    o_ref[...] = jnp.swapaxes(y, 0, 1).reshape(o_ref.shape)


def dft_slow(cmat, x5, t2n=DFT_T2_BLOCK):
    R, K = cmat.shape
    npair, _, half, n2, w = x5.shape
    assert K == 2 * half and n2 % t2n == 0
    return pl.pallas_call(
        _dft_slow_body,
        grid=(npair, n2 // t2n),
        in_specs=[
            pl.BlockSpec((R, K), lambda p, j: (0, 0)),
            pl.BlockSpec((None, 2, half, t2n, w), lambda p, j: (p, 0, 0, j, 0)),
        ],
        out_specs=pl.BlockSpec((None, 2, R // 2, t2n, w), lambda p, j: (p, 0, 0, j, 0)),
        out_shape=jax.ShapeDtypeStruct((npair, 2, R // 2, n2, w), BF16),
        compiler_params=_cp(("parallel", "parallel")),
        name="dft_slow",
    )(cmat, x5)


def _stack_complex(tab_ref, k):
    cr, ci = tab_ref[k, 0], tab_ref[k, 1]
    return jnp.concatenate([jnp.concatenate([cr, -ci], axis=1), jnp.concatenate([ci, cr], axis=1)], axis=0)


def _hy_kf_body(a_ref, cf_ref, e_ref, o_ref, *, F, n_fft):
    n2 = DFT_N2
    scale = lax.rsqrt(e_ref[...] + 1e-6) * (1.0 / n_fft)
    for k in range(F):
        a = jnp.concatenate([a_ref[0, k], a_ref[1, k]], axis=0)
        x = _dot(_stack_complex(cf_ref, k), a) * scale
        o_ref[k, 0] = x[:n2]
        o_ref[k, 1] = x[n2:]


def hy_filter_spectrum(a5, cf_tab, energy, n_fft, F=8):
    n1, n2 = n_fft // DFT_N2, DFT_N2
    F = min(F, n1)
    return pl.pallas_call(
        functools.partial(_hy_kf_body, F=F, n_fft=n_fft),
        grid=(n1 // F,),
        in_specs=[
            pl.BlockSpec((2, F, n2, HY_W), lambda f: (0, f, 0, 0)),
            pl.BlockSpec((F, 2, n2, n2), lambda f: (f, 0, 0, 0)),
            pl.BlockSpec((1, HY_W), lambda f: (0, 0)),
        ],
        out_specs=pl.BlockSpec((F, 2, n2, HY_W), lambda f: (f, 0, 0, 0)),
        out_shape=jax.ShapeDtypeStruct((n1, 2, n2, HY_W), F32),
        compiler_params=_cp(("parallel",)),
        name="hy_kf",
    )(a5, cf_tab, energy)


def _hy_mid_body(a_ref, kf_ref, cf_ref, ci_ref, o_ref, *, F):
    n2 = DFT_N2
    ks = range(F)
    xs = [_dot(_stack_complex(cf_ref, k), jnp.concatenate([a_ref[0, k], a_ref[1, k]], axis=0)) for k in ks]
    ys = []
    for k in ks:
        xr, xi = xs[k][:n2], xs[k][n2:]
        kr, ki = kf_ref[k, 0], kf_ref[k, 1]
        ys.append(jnp.concatenate([xr * kr - xi * ki, xr * ki + xi * kr], axis=0).astype(BF16))
    bvs = [_dot(_stack_complex(ci_ref, k), ys[k]) for k in ks]
    for k in ks:
        o_ref[0, k] = bvs[k][:n2].astype(BF16)
        o_ref[1, k] = bvs[k][n2:].astype(BF16)


def hy_mid(a5, kf, cf_tab, ci_tab, F=8):
    npair = a5.shape[0]
    n1, n2 = kf.shape[0], DFT_N2
    F = min(F, n1)
    return pl.pallas_call(
        functools.partial(_hy_mid_body, F=F),
        grid=(n1 // F, npair),
        in_specs=[
            pl.BlockSpec((None, 2, F, n2, HY_W), lambda f, p: (p, 0, f, 0, 0)),
            pl.BlockSpec((F, 2, n2, HY_W), lambda f, p: (f, 0, 0, 0)),
            pl.BlockSpec((F, 2, n2, n2), lambda f, p: (f, 0, 0, 0)),
            pl.BlockSpec((F, 2, n2, n2), lambda f, p: (f, 0, 0, 0)),
        ],
        out_specs=pl.BlockSpec((None, 2, F, n2, HY_W), lambda f, p: (p, 0, f, 0, 0)),
        out_shape=jax.ShapeDtypeStruct((npair, 2, n1, n2, HY_W), BF16),
        compiler_params=_cp(("parallel", "arbitrary")),
        name="hy_mid",
    )(a5, kf, cf_tab, ci_tab)


def _hy_out_body(c_ref, b_ref, u_ref, x0_ref, skip_ref, o_ref):
    two, n1, t2n, w = b_ref.shape
    b = jnp.swapaxes(b_ref[...].reshape(two * n1, t2n, w), 0, 1)
    c = c_ref[...]
    y = jnp.stack([_dot(c, b[t]) for t in range(t2n)], axis=0)
    y = jnp.swapaxes(y, 0, 1).reshape(o_ref.shape)
    conv = y + skip_ref[...] * u_ref[...].astype(F32)
    o_ref[...] = (x0_ref[...].astype(F32) * conv).astype(BF16)


def hy_out(cmat, bsp, u5, x05, skip, t2n=DFT_T2_BLOCK):
    npair, _, n1, n2, w = bsp.shape
    half = u5.shape[2]
    pair = pl.BlockSpec((None, 2, half, t2n, w), lambda p, j: (p, 0, 0, j, 0))
    return pl.pallas_call(
        _hy_out_body,
        grid=(npair, n2 // t2n),
        in_specs=[
            pl.BlockSpec((2 * half, 2 * n1), lambda p, j: (0, 0)),
            pl.BlockSpec((None, 2, n1, t2n, w), lambda p, j: (p, 0, 0, j, 0)),
            pair, pair,
            pl.BlockSpec((1, w), lambda p, j: (0, 0)),
        ],
        out_specs=pair,
        out_shape=jax.ShapeDtypeStruct((npair, 2, half, n2, w), BF16),
        compiler_params=_cp(("parallel", "parallel")),
        name="hy_out",
    )(cmat, bsp, u5, x05, skip)


@functools.lru_cache(maxsize=None)
def _dft_tables(L):
    n = 2 * L
    n2 = DFT_N2
    n1 = n // n2
    hl = n1 // 2
    f1 = np.arange(n1)
    t1 = np.arange(n1)
    th1 = 2.0 * np.pi * ((f1[:, None] * t1[None, :]) % n1) / n1
    fr, fi = np.cos(th1), -np.sin(th1)
    c1_data = np.block([[fr[:, :hl], -fi[:, :hl]], [fi[:, :hl], fr[:, :hl]]])
    c1_real = np.concatenate([fr, fi], axis=0)
    gr, gi = np.cos(th1.T)[:hl], np.sin(th1.T)[:hl]
    c1_inv = np.block([[gr, -gi], [gi, gr]])
    f2 = np.arange(n2)
    t2 = np.arange(n2)
    k = (t2[None, None, :] * (f1[:, None, None] + n1 * f2[None, :, None])) % n
    th = 2.0 * np.pi * k / n
    cf = np.stack([np.cos(th), -np.sin(th)], axis=1)
    tht = np.transpose(th, (0, 2, 1))
    ci = np.stack([np.cos(tht), np.sin(tht)], axis=1)
    return tuple(a.astype(np.float32).astype(BF16) for a in (c1_data, c1_real, c1_inv, cf, ci))


def hyena_branch(P, B, L, conv_w, conv_b, w1, b1, wm, bm, w3, skip):
    assert B % 2 == 0 and (2 * L) % DFT_N2 == 0 and L % DFT_N2 == 0
    n_fft = 2 * L
    n1, n2 = n_fft // DFT_N2, DFT_N2
    hl = n1 // 2
    NL = n2 * HY_W
    c1_data, c1_real, c1_inv, cf_tab, ci_tab = (jnp.asarray(a) for a in _dft_tables(L))

    hid = LANES
    freqs = np.linspace(1e-4, HY_EMB_BANDS - 1, HY_EMB_BANDS, dtype=np.float32)
    freq_lane = np.zeros((1, LANES), np.float32)
    freq_lane[0, 1:1 + HY_EMB_BANDS] = freqs
    freq_lane[0, 1 + HY_EMB_BANDS:HY_EMB] = freqs
    delta = np.abs(np.linspace(HY_MIN_DECAY, HY_MAX_DECAY, HY_W, dtype=np.float32)).reshape(1, HY_W)
    w1p = jnp.zeros((LANES, hid), F32).at[:HY_EMB, :HY_FILTER_HID].set(w1)
    b1p = jnp.zeros((1, hid), F32).at[0, :HY_FILTER_HID].set(b1)
    wmp = jnp.zeros((HY_INNER_MLPS, hid, hid), F32).at[:, :HY_FILTER_HID, :HY_FILTER_HID].set(wm)
    bmp = jnp.zeros((HY_INNER_MLPS, 1, hid), F32).at[:, 0, :HY_FILTER_HID].set(bm)
    w3p = jnp.zeros((hid, 2 * HY_W), F32).at[:HY_FILTER_HID].set(w3)
    kern, energy = hy_filter(jnp.asarray(freq_lane), jnp.asarray(delta), w1p, b1p, wmp, bmp, w3p, L)
    ak = dft_slow(c1_real, kern.reshape(1, 2, hl, n2, HY_W))[0]
    kf = hy_filter_spectrum(ak, cf_tab, energy, n_fft)

    u, x0c = hy_short_conv(P, conv_w, conv_b.reshape(1, -1), B, L)
    u5 = u.reshape(B // 2, 2, hl, n2, HY_W)
    x05 = x0c.reshape(B // 2, 2, hl, n2, HY_W)
    a = dft_slow(c1_data, u5)
    bsp = hy_mid(a, kf, cf_tab, ci_tab)
    return hy_out(c1_inv, bsp, u5, x05, skip.reshape(1, HY_W)).reshape(B * L, HY_W)


def _merge_body(ona_ref, ohy_ref, ogl_ref, omem_ref, z_ref, g_ref, wbr_ref, wout_ref, bout_ref,
                x_ref, lg_ref, lb_ref, h32_ref, *maybe_h16_ref, alpha):
    D = x_ref.shape[-1]
    branches = range(N_BRANCH)
    o_refs = (ona_ref, ohy_ref, ogl_ref, omem_ref)
    zs = [z_ref[:, j * BRANCH_W:(j + 1) * BRANCH_W] for j in branches]
    acts = [o_refs[j][...] * (zs[j] * (1.0 + jnp.tanh(zs[j]))) for j in branches]
    projs = [_dot(acts[j], wbr_ref[j]) for j in branches]
    gates = [1.0 + jnp.tanh(g_ref[:, j * D:(j + 1) * D].astype(F32)) for j in branches]
    merged = gates[0] * projs[0]
    for j in branches[1:]:
        merged = merged + gates[j] * projs[j]
    out = _dot(merged.astype(BF16), wout_ref[...]) + bout_ref[...]
    h = _layer_norm_rows(alpha * x_ref[...] + out, lg_ref[...], lb_ref[...])
    h32_ref[...] = h
    for h16_ref in maybe_h16_ref:
        h16_ref[...] = h.astype(BF16)


def merge_layer(o_na, o_hy, o_gla, o_mem, P, wbr, wout, bout, x32, lg, lb, alpha, want_bf16, tm=512):
    M, D = x32.shape
    row = lambda w, blk=0: pl.BlockSpec((tm, w), lambda i, _b=blk: (i, _b))
    vec = pl.BlockSpec((1, D), lambda i: (0, 0))
    n_out = 2 if want_bf16 else 1
    outs = pl.pallas_call(
        functools.partial(_merge_body, alpha=alpha),
        grid=(M // tm,),
        in_specs=[
            row(BRANCH_W), row(BRANCH_W), row(BRANCH_W), row(BRANCH_W),
            row(N_BRANCH * BRANCH_W, COL_Z // (N_BRANCH * BRANCH_W)),
            row(N_BRANCH * D, COL_G // (N_BRANCH * D)),
            pl.BlockSpec((N_BRANCH, BRANCH_W, D), lambda i: (0, 0, 0), pipeline_mode=pl.Buffered(1)),
            pl.BlockSpec((D, D), lambda i: (0, 0), pipeline_mode=pl.Buffered(1)),
            vec, row(D), vec, vec,
        ],
        out_specs=[row(D), row(D)][:n_out],
        out_shape=[jax.ShapeDtypeStruct((M, D), F32), jax.ShapeDtypeStruct((M, D), BF16)][:n_out],
        compiler_params=_cp(("parallel",)),
        name="merge",
    )(o_na, o_hy, o_gla, o_mem, P, P, wbr, wout, bout, x32, lg, lb)
    return (outs[0], outs[1]) if want_bf16 else (outs[0], None)


def _permute_in_proj(w_in, b_in):
    sizes = (512, 512, 512, 1536, 256, 256, 512, GLA_RANK, GLA_RANK, 512, 2048, 4096)
    offs = np.concatenate([[0], np.cumsum(sizes)])
    seg = lambda a, i: a[..., offs[i]:offs[i + 1]]
    order = (11, 10, 0, 1, 2, 3, 4, 5, 6, 9)
    col_scale = {0: NA_QSCALE, 10: 0.5, 11: 0.5}
    scaled = lambda a, i: seg(a, i) * col_scale[i] if i in col_scale else seg(a, i)
    wm = jnp.concatenate([scaled(w_in, i) for i in order], axis=-1).astype(BF16)
    bm = jnp.concatenate([scaled(b_in, i) for i in order], axis=-1).reshape(1, -1)
    pad = LR_PAD - 2 * GLA_RANK
    wlr = jnp.pad(jnp.concatenate([seg(w_in, 7), seg(w_in, 8)], axis=-1), ((0, 0), (0, pad))).astype(BF16)
    blr = jnp.pad(jnp.concatenate([seg(b_in, 7), seg(b_in, 8)], axis=-1), ((0, pad),)).reshape(1, -1)
    return wm, bm, wlr, blr


def _hybrid_layer(h32, h16, mem16, B, L, depth, want_bf16, w_in, b_in, na_rpb, hy_conv_w, hy_conv_b, hy_w1, hy_b1,
                  hy_wm, hy_bm, hy_w3, hy_skip, gla_up_f, gla_bias_f, gla_up_b, gla_bias_b, gla_norm_g,
                  mem_w_kv, w_br, w_out, b_out, ln_g, ln_b):
    D = h32.shape[-1]
    rows = L // GRID_W
    assert rows % NA_G == 0 and rows >= NA_KR and L % GLA_T == 0
    wm, bm, wlr, blr = _permute_in_proj(w_in, b_in)
    assert wm.shape[1] == N_MAIN
    P, LR = in_proj(h16, wm, bm, wlr, blr)

    ng = rows // NA_G
    geo = [_na_group_geometry(g, rows) for g in range(ng)]
    assert all(geo[g][2] == geo[1][2] for g in range(1, ng - 1))
    rpb_flat = na_rpb.reshape(-1)
    bias = jnp.stack([na_bias_tile(rpb_flat, geo[g][2]) for g in (0, 1, ng - 1)], axis=0)
    o_na = na_attention(P, bias, B, L)

    o_hy = hyena_branch(P, B, L, hy_conv_w, hy_conv_b, hy_w1, hy_b1, hy_wm, hy_bm, hy_w3, hy_skip)

    HK = GLA_HEADS * GLA_DK
    up_f = jnp.zeros((LR_PAD, HK), F32).at[:GLA_RANK].set(gla_up_f)
    up_b = jnp.zeros((LR_PAD, HK), F32).at[GLA_RANK:2 * GLA_RANK].set(gla_up_b)
    o_b = gla_direction(P, LR, up_b, gla_bias_b.reshape(1, HK), B, L, reverse=True)
    o_gla = gla_direction(P, LR, up_f, gla_bias_f.reshape(1, HK), B, L, reverse=False,
                          o_other=o_b, norm_g=gla_norm_g.reshape(1, -1))

    n_mem = mem16.shape[0] // B
    kv = matmul_bf16(mem16, mem_w_kv.astype(BF16))
    o_mem = mem_attention(P, kv, B, L, n_mem)

    alpha = (2 * depth) ** 0.25
    return merge_layer(o_na, o_hy, o_gla, o_mem, P, (0.5 * w_br).astype(BF16), w_out.astype(BF16),
                       b_out.reshape(1, D), h32, ln_g.reshape(1, D), ln_b.reshape(1, D), alpha, want_bf16)


def kernel(x, mem, ln_in_g, ln_in_b, w_in, b_in, na_rpb, hy_conv_w, hy_conv_b, hy_w1, hy_b1, hy_wm, hy_bm,
           hy_w3, hy_skip, gla_up_f, gla_bias_f, gla_up_b, gla_bias_b, gla_norm_g, mem_w_kv, w_br, w_out,
           b_out, ln_g, ln_b):
    B, L, D = x.shape
    depth = w_in.shape[0]
    h32, h16 = ln_in(x.reshape(B * L, D), ln_in_g, ln_in_b)
    mem16 = mem.reshape(-1, D).astype(BF16)
    for i in range(depth):
        h32, h16 = _hybrid_layer(
            h32, h16, mem16, B, L, depth, i + 1 < depth, w_in[i], b_in[i], na_rpb[i], hy_conv_w[i], hy_conv_b[i], hy_w1[i],
            hy_b1[i], hy_wm[i], hy_bm[i], hy_w3[i], hy_skip[i], gla_up_f[i], gla_bias_f[i], gla_up_b[i],
            gla_bias_b[i], gla_norm_g[i], mem_w_kv[i], w_br[i], w_out[i], b_out[i], ln_g[i], ln_b[i])
    return h32.reshape(B, L, D)
```
